```python
import math
import jax, jax.numpy as jnp
from jax import lax
import numpy as np


D_MODEL = 4096
BATCH = 4
SEQ = 2048
DEPTH = 1

HEAD_DIM = 128
N_ATTN_HEADS = 24
ATTN_WIDTH = N_ATTN_HEADS * HEAD_DIM
CONV_WIDTH = D_MODEL - ATTN_WIDTH
CONV_KERNEL = 3
DILATED_PATTERNS = ((128, 1), (512, 4), (2048, 16))
IN_PROJ_WIDTH = 3 * ATTN_WIDTH + 3 * CONV_WIDTH
N_EXPERTS = 32
TOP_K = 4
D_EXPERT = D_MODEL // 2
SWIGLU_LIMIT = 7.0
SWIGLU_ALPHA = 1.702
MOE_ROW_BLOCK = 128
LN_EPS = 1e-5
DEEPNORM_ALPHA = (2 * DEPTH) ** 0.25
DEEPNORM_BETA = (8 * DEPTH) ** -0.25

kernel_name = "hymba_dilated_conv_moe_deepnorm"


def _alibi_slopes(n):
    def pow2(m):
        start = 2.0 ** (-(2.0 ** -(math.log2(m) - 3)))
        return [start ** (i + 1) for i in range(m)]
    if math.log2(n).is_integer():
        s = pow2(n)
    else:
        c = 2 ** int(math.floor(math.log2(n)))
        s = pow2(c) + pow2(2 * c)[0::2][: n - c]
    return np.asarray(s, dtype=np.float32)


def _layer_norm(x, g, b):
    xf = x.astype(jnp.float32)
    mu = jnp.mean(xf, axis=-1, keepdims=True)
    xc = xf - mu
    var = jnp.mean(xc * xc, axis=-1, keepdims=True)
    y = xc * lax.rsqrt(var + LN_EPS) * g.astype(jnp.float32) + b.astype(jnp.float32)
    return y.astype(x.dtype)


def _dilated_branch(q, k, v, slopes, window, dilation):
    bsz, seq, heads, hd = q.shape
    blk = window // dilation
    L = seq // dilation
    nb = -(-L // blk)
    Lp = nb * blk

    def to_blocks(t):
        t = t.reshape(bsz, L, dilation, heads, hd).transpose(0, 2, 1, 3, 4)
        t = jnp.pad(t, ((0, 0), (0, 0), (0, Lp - L), (0, 0), (0, 0)))
        return t.reshape(bsz, dilation, nb, blk, heads, hd)

    def with_prev(t):
        prev = jnp.concatenate([jnp.zeros_like(t[:, :, :1]), t[:, :, :-1]], axis=2)
        return jnp.concatenate([prev, t], axis=3)

    qb = to_blocks(q)
    kk = with_prev(to_blocks(k))
    vv = with_prev(to_blocks(v))

    qi = jnp.arange(blk)[:, None]
    ki = jnp.arange(2 * blk)[None, :]
    steps = qi + blk - ki
    band = (steps >= 0) & (steps <= blk)
    not_before_start = (jnp.arange(nb)[:, None, None] > 0) | (ki >= blk)[None]
    mask = band[None] & not_before_start
    bias = -slopes[:, None, None] * (steps * dilation).astype(jnp.float32)[None]

    scale = 1.0 / math.sqrt(hd)
    s = jnp.einsum('brnqhd,brnkhd->brnhqk', qb, kk, preferred_element_type=jnp.float32)
    s = s * scale + bias[None, None, None]
    s = jnp.where(mask[None, None, :, None], s, -jnp.inf)
    m = jnp.max(s, axis=-1, keepdims=True)
    p = jnp.exp(s - m)
    den = jnp.sum(p, axis=-1)
    o = jnp.einsum('brnhqk,brnkhd->brnqhd', p, vv.astype(jnp.float32))
    den_q = jnp.swapaxes(den, 3, 4)
    o = o / den_q[..., None]
    lse = jnp.swapaxes(m[..., 0] + jnp.log(den), 3, 4)

    o = o.reshape(bsz, dilation, Lp, heads, hd)[:, :, :L]
    o = o.transpose(0, 2, 1, 3, 4).reshape(bsz, seq, heads, hd)
    lse = lse.reshape(bsz, dilation, Lp, heads)[:, :, :L]
    lse = lse.transpose(0, 2, 1, 3).reshape(bsz, seq, heads)
    return o, lse


def _hybrid_mixer(h, w_in, conv_w, w_out, slopes):
    bsz, seq, _ = h.shape
    A, C = ATTN_WIDTH, CONV_WIDTH
    proj = h @ w_in
    q, k, v, gate_b, gate_c, val = jnp.split(
        proj, [A, 2 * A, 3 * A, 3 * A + C, 3 * A + 2 * C], axis=-1)
    q = q.reshape(bsz, seq, N_ATTN_HEADS, HEAD_DIM)
    k = k.reshape(bsz, seq, N_ATTN_HEADS, HEAD_DIM)
    v = v.reshape(bsz, seq, N_ATTN_HEADS, HEAD_DIM)

    outs, lses = [], []
    for window, dilation in DILATED_PATTERNS:
        o, l = _dilated_branch(q, k, v, slopes, window, dilation)
        outs.append(o)
        lses.append(l)
    wts = jax.nn.softmax(jnp.stack(lses), axis=0)
    attn = jnp.einsum('gbsh,gbshd->bshd', wts, jnp.stack(outs))
    attn = attn.astype(h.dtype).reshape(bsz, seq, A)

    u = gate_c * val
    z = lax.conv_general_dilated(
        u, conv_w.astype(u.dtype)[:, None, :], window_strides=(1,),
        padding=[(CONV_KERNEL - 1, 0)], dimension_numbers=('NWC', 'WIO', 'NWC'),
        feature_group_count=C)
    y_conv = gate_b * z

    return jnp.concatenate([attn, y_conv], axis=-1) @ w_out


def _moe(h, router_w, router_b, w_gu, b_gu, w_dn, b_dn):
    bsz, seq, d = h.shape
    T = bsz * seq
    A = T * TOP_K
    xf = h.reshape(T, d)
    logits = (xf @ router_w).astype(jnp.float32) + router_b.astype(jnp.float32)
    top_vals, top_idx = lax.top_k(logits, TOP_K)
    gates = jax.nn.softmax(top_vals, axis=-1)

    flat_e = top_idx.reshape(A)
    flat_tok = jnp.repeat(jnp.arange(T, dtype=jnp.int32), TOP_K)
    flat_g = gates.reshape(A)
    order = jnp.argsort(flat_e)
    sorted_e = flat_e[order]
    counts = jnp.bincount(flat_e, length=N_EXPERTS)
    padded = (counts + MOE_ROW_BLOCK - 1) // MOE_ROW_BLOCK * MOE_ROW_BLOCK
    starts = jnp.cumsum(counts) - counts
    pends = jnp.cumsum(padded)
    pstarts = pends - padded
    dest = pstarts[sorted_e] + jnp.arange(A) - starts[sorted_e]

    n_blocks = -(-A // MOE_ROW_BLOCK) + N_EXPERTS
    P = n_blocks * MOE_ROW_BLOCK
    row_tok = jnp.zeros((P,), jnp.int32).at[dest].set(flat_tok[order])
    row_gate = jnp.zeros((P,), jnp.float32).at[dest].set(flat_g[order])
    block_e = jnp.minimum(
        jnp.searchsorted(pends, jnp.arange(n_blocks) * MOE_ROW_BLOCK, side='right'),
        N_EXPERTS - 1)
    xs = xf[row_tok].reshape(n_blocks, MOE_ROW_BLOCK, d)

    def expert_block(args):
        xb, e = args
        gu = xb @ w_gu[e] + b_gu[e]
        g_, u_ = jnp.split(gu, 2, axis=-1)
        g_ = jnp.minimum(g_, SWIGLU_LIMIT)
        u_ = jnp.clip(u_, -SWIGLU_LIMIT, SWIGLU_LIMIT)
        act = g_ * jax.nn.sigmoid(SWIGLU_ALPHA * g_)
        return ((u_ + 1.0) * act) @ w_dn[e] + b_dn[e]

    ys = lax.map(expert_block, (xs, block_e)).reshape(P, d)
    y = jax.ops.segment_sum(ys * row_gate[:, None].astype(ys.dtype), row_tok, num_segments=T)
    return y.reshape(bsz, seq, d)


def setup_inputs(seed: int = 0) -> dict:
    key = jax.random.key(seed)
    ks = jax.random.split(key, 20)
    f32 = jnp.float32
    D, A, C, E, F = D_MODEL, ATTN_WIDTH, CONV_WIDTH, N_EXPERTS, D_EXPERT
    nrm = lambda k, shp: jax.random.normal(k, shp, f32)
    din = D ** -0.5
    x = nrm(ks[0], (BATCH, SEQ, D))
    ln_in_g = 1.0 + 0.02 * nrm(ks[1], (D,))
    ln_in_b = 0.02 * nrm(ks[2], (D,))
    kq, kk_, kv, kb, kc, kval = jax.random.split(ks[3], 6)
    w_in = jnp.concatenate([
        nrm(kq, (DEPTH, D, A)) * din,
        nrm(kk_, (DEPTH, D, A)) * din,
        nrm(kv, (DEPTH, D, A)) * din * DEEPNORM_BETA,
        nrm(kb, (DEPTH, D, C)) * din,
        nrm(kc, (DEPTH, D, C)) * din,
        nrm(kval, (DEPTH, D, C)) * din * DEEPNORM_BETA,
    ], axis=-1)
    conv_w = nrm(ks[4], (DEPTH, CONV_KERNEL, C)) * CONV_KERNEL ** -0.5
    w_out = nrm(ks[5], (DEPTH, D, D)) * din * DEEPNORM_BETA
    ln1_g = 1.0 + 0.02 * nrm(ks[6], (DEPTH, D))
    ln1_b = 0.02 * nrm(ks[7], (DEPTH, D))
    router_w = nrm(ks[8], (DEPTH, D, E)) * din
    router_b = 0.01 * nrm(ks[9], (DEPTH, E))
    w_gate_up = nrm(ks[10], (DEPTH, E, D, 2 * F)) * din * DEEPNORM_BETA
    b_gate_up = 0.01 * nrm(ks[11], (DEPTH, E, 2 * F))
    w_down = nrm(ks[12], (DEPTH, E, F, D)) * F ** -0.5 * DEEPNORM_BETA
    b_down = 0.01 * nrm(ks[13], (DEPTH, E, D))
    ln2_g = 1.0 + 0.02 * nrm(ks[14], (DEPTH, D))
    ln2_b = 0.02 * nrm(ks[15], (DEPTH, D))
    return {"x": x, "ln_in_g": ln_in_g, "ln_in_b": ln_in_b, "w_in": w_in,
            "conv_w": conv_w, "w_out": w_out, "ln1_g": ln1_g, "ln1_b": ln1_b,
            "router_w": router_w, "router_b": router_b, "w_gate_up": w_gate_up,
            "b_gate_up": b_gate_up, "w_down": w_down, "b_down": b_down,
            "ln2_g": ln2_g, "ln2_b": ln2_b}


def reference(x, ln_in_g, ln_in_b, w_in, conv_w, w_out, ln1_g, ln1_b, router_w, router_b,
              w_gate_up, b_gate_up, w_down, b_down, ln2_g, ln2_b):
    slopes = jnp.asarray(_alibi_slopes(N_ATTN_HEADS))
    h = _layer_norm(x, ln_in_g, ln_in_b)
    for layer in range(DEPTH):
        mix = _hybrid_mixer(h, w_in[layer], conv_w[layer], w_out[layer], slopes)
        h = _layer_norm(DEEPNORM_ALPHA * h + mix, ln1_g[layer], ln1_b[layer])
        ffn = _moe(h, router_w[layer], router_b[layer], w_gate_up[layer], b_gate_up[layer],
                   w_down[layer], b_down[layer])
        h = _layer_norm(DEEPNORM_ALPHA * h + ffn, ln2_g[layer], ln2_b[layer])
    return h
```

```python
import dataclasses
import functools
import math

import numpy as np
import jax
import jax.numpy as jnp
from jax import lax
from jax.experimental import pallas as pl
from jax.experimental.pallas import tpu as pltpu

F32 = jnp.float32
BF16 = jnp.bfloat16

LN_EPS = 1e-5
SWIGLU_LIMIT = 7.0
SWIGLU_ALPHA = 1.702
MASKED_SCORE = -1e30
V7X_VMEM_LIMIT_BYTES = 56 * 1024 * 1024


@dataclasses.dataclass(frozen=True)
class Config:
    batch: int
    seq: int
    d_model: int
    head_dim: int
    n_heads: int
    conv_width: int
    patterns: tuple
    n_experts: int
    top_k: int
    d_expert: int
    depth: int
    ln_rows: int
    mm_tm: int
    mm_tn: int
    conv_cw: int
    moe_rows: int
    gu_tf: int
    dn_tn: int
    gather_rows: int
    combine_rows: int

    @property
    def attn_width(self):
        return self.n_heads * self.head_dim

    @property
    def tokens(self):
        return self.batch * self.seq


def _params(sem, vmem=V7X_VMEM_LIMIT_BYTES):
    return pltpu.CompilerParams(dimension_semantics=sem, vmem_limit_bytes=vmem)


def _alibi_slopes(n):
    def pow2(m):
        start = 2.0 ** (-(2.0 ** -(math.log2(m) - 3)))
        return [start ** (i + 1) for i in range(m)]
    if math.log2(n).is_integer():
        s = pow2(n)
    else:
        c = 2 ** int(math.floor(math.log2(n)))
        s = pow2(c) + pow2(2 * c)[0::2][: n - c]
    return np.asarray(s, dtype=np.float32)


def _layer_norm_rows(x, g, b):
    mu = jnp.mean(x, axis=-1, keepdims=True)
    xc = x - mu
    var = jnp.mean(xc * xc, axis=-1, keepdims=True)
    return xc * lax.rsqrt(var + LN_EPS) * g + b


def _ln_in_kernel(x_ref, g_ref, b_ref, o_ref):
    o_ref[...] = _layer_norm_rows(x_ref[...], g_ref[...], b_ref[...]).astype(o_ref.dtype)


def _ln_in(x2d, g, b, cfg):
    t, d = x2d.shape
    tm = cfg.ln_rows
    row = pl.BlockSpec((tm, d), lambda i: (i, 0))
    vec = pl.BlockSpec((1, d), lambda i: (0, 0))
    return pl.pallas_call(
        _ln_in_kernel, grid=(t // tm,), in_specs=[row, vec, vec], out_specs=row,
        out_shape=jax.ShapeDtypeStruct((t, d), BF16),
        compiler_params=_params(("parallel",)), name="ln_in",
    )(x2d, g.reshape(1, d), b.reshape(1, d))


def _matmul_kernel(x_ref, w_ref, o_ref):
    w = w_ref[...].astype(BF16)
    o_ref[...] = jnp.dot(x_ref[...], w, preferred_element_type=F32).astype(o_ref.dtype)


def _matmul(x, w, cfg, out_dtype, name):
    m, k = x.shape
    n = w.shape[1]
    tm, tn = cfg.mm_tm, cfg.mm_tn
    return pl.pallas_call(
        _matmul_kernel, grid=(m // tm, n // tn),
        in_specs=[pl.BlockSpec((tm, k), lambda i, j: (i, 0)),
                  pl.BlockSpec((k, tn), lambda i, j: (0, j))],
        out_specs=pl.BlockSpec((tm, tn), lambda i, j: (i, j)),
        out_shape=jax.ShapeDtypeStruct((m, n), out_dtype),
        compiler_params=_params(("parallel", "parallel")), name=name,
    )(x, w)


def _attn_kernel(slopes_ref, q_ref, k_ref, v_ref, o_ref, acc_ref, m_ref, l_ref, *, cfg, blk):
    seq, hd = cfg.seq, cfg.head_dim
    slope = slopes_ref[pl.program_id(1)]
    scale = 1.0 / math.sqrt(hd)

    qi = lax.broadcasted_iota(jnp.int32, (blk, blk), 0)
    ki = lax.broadcasted_iota(jnp.int32, (blk, blk), 1)
    d_own = qi - ki
    own_valid = d_own >= 0
    prev_valid = d_own <= 0
    own_steps = d_own.astype(F32)
    prev_steps = (d_own + blk).astype(F32)

    m_ref[...] = jnp.full(m_ref.shape, MASKED_SCORE, F32)
    l_ref[...] = jnp.zeros(l_ref.shape, F32)
    acc_ref[...] = jnp.zeros(acc_ref.shape, F32)

    def rows(start, dil):
        return pl.ds(start, blk, stride=dil) if dil > 1 else pl.ds(start, blk)

    def scores(q, k_rows, bias):
        k = k_ref[k_rows, :].astype(BF16)
        s = lax.dot_general(q, k, (((1,), (1,)), ((), ())), preferred_element_type=F32)
        return s + bias

    def update(start, dil, with_prev):
        neg = -slope * float(dil)
        rq = rows(start, dil)
        q = (q_ref[rq, :] * scale).astype(BF16)
        s_own = scores(q, rq, jnp.where(own_valid, own_steps * neg, MASKED_SCORE))
        m_cur = jnp.max(s_own, axis=-1, keepdims=True)
        if with_prev:
            rp = rows(start - blk * dil, dil)
            s_prev = scores(q, rp, jnp.where(prev_valid, prev_steps * neg, MASKED_SCORE))
            m_cur = jnp.maximum(m_cur, jnp.max(s_prev, axis=-1, keepdims=True))
        m_old = m_ref[rq, :]
        m_new = jnp.maximum(m_old, m_cur)
        alpha = jnp.exp(m_old - m_new)
        m_keys = m_new if blk == hd else m_new[:, :1]
        p = jnp.exp(s_own - m_keys)
        l_add = jnp.sum(p, axis=-1, keepdims=True)
        pv = jnp.dot(p.astype(BF16), v_ref[rq, :].astype(BF16), preferred_element_type=F32)
        if with_prev:
            p = jnp.exp(s_prev - m_keys)
            l_add = l_add + jnp.sum(p, axis=-1, keepdims=True)
            pv = pv + jnp.dot(p.astype(BF16), v_ref[rp, :].astype(BF16), preferred_element_type=F32)
        l_ref[rq, :] = alpha * l_ref[rq, :] + l_add
        acc_ref[rq, :] = alpha * acc_ref[rq, :] + pv
        m_ref[rq, :] = m_new

    for window, dil in cfg.patterns:
        nb = seq // dil // blk
        if dil == 1:
            update(0, 1, False)
        else:
            def first(r, c, dil=dil):
                update(r, dil, False)
                return c
            lax.fori_loop(0, dil, first, 0)
        if nb > 1:
            shift = dil.bit_length() - 1

            def later(idx, c, dil=dil, shift=shift):
                r = idx & (dil - 1)
                n = 1 + (idx >> shift)
                start = r + n * (blk * dil)
                if dil == 1:
                    start = pl.multiple_of(start, blk)
                update(start, dil, True)
                return c
            lax.fori_loop(0, dil * (nb - 1), later, 0)

    def finish(c, carry):
        r = pl.ds(pl.multiple_of(c * blk, blk), blk)
        o_ref[r, :] = (acc_ref[r, :] / l_ref[r, :]).astype(o_ref.dtype)
        return carry
    lax.fori_loop(0, seq // blk, finish, 0)


def _attention(proj, slopes, cfg):
    seq, hd, nh = cfg.seq, cfg.head_dim, cfg.n_heads
    blks = {w // d for w, d in cfg.patterns}
    assert len(blks) == 1, "all windows must share one block length"
    blk = blks.pop()
    for w, d in cfg.patterns:
        assert d & (d - 1) == 0 and seq % (d * blk) == 0
    grid_spec = pltpu.PrefetchScalarGridSpec(
        num_scalar_prefetch=1, grid=(cfg.batch, nh),
        in_specs=[pl.BlockSpec((seq, hd), lambda b, h, s: (b, h)),
                  pl.BlockSpec((seq, hd), lambda b, h, s: (b, nh + h)),
                  pl.BlockSpec((seq, hd), lambda b, h, s: (b, 2 * nh + h))],
        out_specs=pl.BlockSpec((seq, hd), lambda b, h, s: (b, h)),
        scratch_shapes=[pltpu.VMEM((seq, hd), F32)] * 3)
    return pl.pallas_call(
        functools.partial(_attn_kernel, cfg=cfg, blk=blk), grid_spec=grid_spec,
        out_shape=jax.ShapeDtypeStruct((cfg.tokens, cfg.attn_width), BF16),
        compiler_params=_params(("parallel", "parallel")), name="dilated_attention",
    )(slopes, proj, proj, proj)


def _conv_kernel(gb_ref, gc_ref, val_ref, w_ref, o_ref):
    u = gc_ref[...] * val_ref[...]
    row = lax.broadcasted_iota(jnp.int32, u.shape, 0)
    u1 = jnp.where(row >= 1, pltpu.roll(u, 1, 0), 0.0)
    u2 = jnp.where(row >= 2, pltpu.roll(u, 2, 0), 0.0)
    w = w_ref[...]
    z = w[0:1, :] * u2 + w[1:2, :] * u1 + w[2:3, :] * u
    o_ref[...] = (gb_ref[...] * z).astype(o_ref.dtype)


def _gated_conv(proj, conv_w, cfg):
    seq, cw, c = cfg.seq, cfg.conv_cw, cfg.conv_width
    base = 3 * cfg.attn_width // cw
    per = c // cw
    spec = lambda off: pl.BlockSpec((seq, cw), lambda b, j, off=off: (b, base + off * per + j))
    return pl.pallas_call(
        _conv_kernel, grid=(cfg.batch, per),
        in_specs=[spec(0), spec(1), spec(2), pl.BlockSpec((conv_w.shape[0], cw), lambda b, j: (0, j))],
        out_specs=pl.BlockSpec((seq, cw), lambda b, j: (b, j)),
        out_shape=jax.ShapeDtypeStruct((cfg.tokens, c), BF16),
        compiler_params=_params(("parallel", "parallel")), name="gated_conv",
    )(proj, proj, proj, conv_w)


def _ln1_router_kernel(x_ref, mix_ref, gi_ref, bi_ref, g1_ref, b1_ref, rw_ref, rb_ref,
                       h_ref, idx_ref, gate_ref, *, alpha, top_k):
    h0 = _layer_norm_rows(x_ref[...], gi_ref[...], bi_ref[...])
    h1 = _layer_norm_rows(alpha * h0 + mix_ref[...], g1_ref[...], b1_ref[...])
    h_ref[...] = h1
    logits = lax.dot_general(rw_ref[...], h1, (((1,), (1,)), ((), ())),
                             precision=lax.Precision.HIGHEST, preferred_element_type=F32)
    logits = logits + rb_ref[...]
    n_e = logits.shape[0]
    eidx = lax.broadcasted_iota(jnp.int32, logits.shape, 0)
    vals, idxs = [], []
    for _ in range(top_k):
        m = jnp.max(logits, axis=0, keepdims=True)
        i = jnp.min(jnp.where(logits == m, eidx, n_e), axis=0, keepdims=True)
        vals.append(m)
        idxs.append(i)
        logits = jnp.where(eidx == i, -jnp.inf, logits)
    ex = [jnp.exp(v - vals[0]) for v in vals]
    den = functools.reduce(lambda a, b: a + b, ex)
    idx_ref[...] = jnp.concatenate(idxs, axis=0)
    gate_ref[...] = jnp.concatenate([e / den for e in ex], axis=0)


def _ln1_router(x2d, mix, gi, bi, g1, b1, router_w, router_b, cfg):
    t, d = x2d.shape
    e, k = cfg.n_experts, cfg.top_k
    tm = cfg.ln_rows
    alpha = (2 * cfg.depth) ** 0.25
    row = pl.BlockSpec((tm, d), lambda i: (i, 0))
    vec = pl.BlockSpec((1, d), lambda i: (0, 0))
    sel = pl.BlockSpec((k, tm), lambda i: (0, i))
    return pl.pallas_call(
        functools.partial(_ln1_router_kernel, alpha=alpha, top_k=k), grid=(t // tm,),
        in_specs=[row, row, vec, vec, vec, vec,
                  pl.BlockSpec((e, d), lambda i: (0, 0)), pl.BlockSpec((e, 1), lambda i: (0, 0))],
        out_specs=[row, sel, sel],
        out_shape=[jax.ShapeDtypeStruct((t, d), F32), jax.ShapeDtypeStruct((k, t), jnp.int32),
                   jax.ShapeDtypeStruct((k, t), F32)],
        compiler_params=_params(("parallel",)), name="ln1_router",
    )(x2d, mix, gi.reshape(1, d), bi.reshape(1, d), g1.reshape(1, d), b1.reshape(1, d),
      router_w.T, router_b.reshape(e, 1))


def _routing_plan(top_idx, cfg):
    k, t = top_idx.shape
    e, rb = cfg.n_experts, cfg.moe_rows
    a = k * t
    nb = a // rb + e
    p = nb * rb
    flat_e = top_idx.reshape(a)
    onehot = (flat_e[:, None] == jnp.arange(e, dtype=jnp.int32)[None, :]).astype(jnp.int32)
    csum = jnp.cumsum(onehot, axis=0)
    counts = csum[-1]
    rank = jnp.take_along_axis(csum, flat_e[:, None], axis=1)[:, 0] - 1
    padded = (counts + rb - 1) // rb * rb
    pends = jnp.cumsum(padded)
    pstarts = pends - padded
    starts = jnp.cumsum(counts) - counts
    pos = pstarts[flat_e] + rank
    order = jnp.argsort(flat_e, stable=True)
    rows = jnp.arange(p, dtype=jnp.int32)
    row_e = jnp.minimum(jnp.sum((rows[:, None] >= pends[None, :]).astype(jnp.int32), axis=1), e - 1)
    local = rows - pstarts[row_e]
    src = jnp.clip(starts[row_e] + local, 0, a - 1)
    row_tok = jnp.where(local < counts[row_e], order[src] % t, 0).astype(jnp.int32)
    block_e = row_e[::rb]
    first = jnp.concatenate([jnp.ones((1,), jnp.int32),
                             (block_e[1:] != block_e[:-1]).astype(jnp.int32)])
    n_blocks = (pends[-1] // rb).astype(jnp.int32).reshape(1)
    return row_tok, pos.reshape(k, t).astype(jnp.int32), block_e.astype(jnp.int32), first, n_blocks


def _row_copy(src_hbm, dst_vmem, sem, src_row, dst_row):
    return pltpu.make_async_copy(src_hbm.at[pl.ds(src_row, 1), :], dst_vmem.at[pl.ds(dst_row, 1), :], sem)


def _gather_kernel(tok_ref, h_hbm, o_ref, buf, sem):
    n = buf.shape[0]

    def issue(r, c):
        _row_copy(h_hbm, buf, sem, tok_ref[0, r], r).start()
        return c
    lax.fori_loop(0, n, issue, 0)

    def drain(r, c):
        _row_copy(h_hbm, buf, sem, 0, r).wait()
        return c
    lax.fori_loop(0, n, drain, 0)
    o_ref[...] = buf[...].astype(o_ref.dtype)


def _gather_rows(h1, row_tok, cfg):
    p = row_tok.shape[0]
    d = h1.shape[1]
    r = cfg.gather_rows
    return pl.pallas_call(
        _gather_kernel, grid=(p // r,),
        in_specs=[pl.BlockSpec((None, 1, r), lambda i: (i, 0, 0), memory_space=pltpu.SMEM),
                  pl.BlockSpec(memory_space=pl.ANY)],
        out_specs=pl.BlockSpec((r, d), lambda i: (i, 0)),
        out_shape=jax.ShapeDtypeStruct((p, d), BF16),
        scratch_shapes=[pltpu.VMEM((r, d), F32), pltpu.SemaphoreType.DMA],
        compiler_params=_params(("arbitrary",)), name="moe_gather",
    )(row_tok.reshape(p // r, 1, r), h1)


def _gate_up_kernel(be_ref, first_ref, nb_ref, x_ref, wg_ref, wu_ref, bg_ref, bu_ref, o_ref, wg_s, wu_s):
    i = pl.program_id(1)

    @pl.when(first_ref[i] == 1)
    def _():
        wg_s[...] = wg_ref[...].astype(BF16)
        wu_s[...] = wu_ref[...].astype(BF16)

    @pl.when(i < nb_ref[0])
    def _():
        x = x_ref[...]
        g = jnp.dot(x, wg_s[...], preferred_element_type=F32) + bg_ref[...]
        u = jnp.dot(x, wu_s[...], preferred_element_type=F32) + bu_ref[...]
        g = jnp.minimum(g, SWIGLU_LIMIT)
        u = jnp.clip(u, -SWIGLU_LIMIT, SWIGLU_LIMIT)
        act = g * jax.nn.sigmoid(SWIGLU_ALPHA * g)
        o_ref[...] = ((u + 1.0) * act).astype(o_ref.dtype)

    @pl.when(i >= nb_ref[0])
    def _():
        o_ref[...] = jnp.zeros(o_ref.shape, o_ref.dtype)


def _gate_up(xs, w_gu, b_gu, block_e, first, n_blocks, cfg):
    p, d = xs.shape
    e, f, rb, tf = cfg.n_experts, cfg.d_expert, cfg.moe_rows, cfg.gu_tf
    nj = f // tf
    grid_spec = pltpu.PrefetchScalarGridSpec(
        num_scalar_prefetch=3, grid=(nj, p // rb),
        in_specs=[pl.BlockSpec((rb, d), lambda j, i, be, fi, nb: (i, 0)),
                  pl.BlockSpec((None, d, tf), lambda j, i, be, fi, nb: (be[i], 0, j)),
                  pl.BlockSpec((None, d, tf), lambda j, i, be, fi, nb: (be[i], 0, nj + j)),
                  pl.BlockSpec((None, 1, tf), lambda j, i, be, fi, nb: (be[i], 0, j)),
                  pl.BlockSpec((None, 1, tf), lambda j, i, be, fi, nb: (be[i], 0, nj + j))],
        out_specs=pl.BlockSpec((rb, tf), lambda j, i, be, fi, nb: (i, j)),
        scratch_shapes=[pltpu.VMEM((d, tf), BF16), pltpu.VMEM((d, tf), BF16)])
    b3 = b_gu.reshape(e, 1, 2 * f)
    return pl.pallas_call(
        _gate_up_kernel, grid_spec=grid_spec,
        out_shape=jax.ShapeDtypeStruct((p, f), BF16),
        compiler_params=_params(("arbitrary", "arbitrary")), name="moe_gate_up",
    )(block_e, first, n_blocks, xs, w_gu, w_gu, b3, b3)


def _down_kernel(be_ref, first_ref, nb_ref, a_ref, w_ref, b_ref, o_ref, w_s):
    i = pl.program_id(1)

    @pl.when(first_ref[i] == 1)
    def _():
        w_s[...] = w_ref[...].astype(BF16)

    @pl.when(i < nb_ref[0])
    def _():
        o_ref[...] = jnp.dot(a_ref[...], w_s[...], preferred_element_type=F32) + b_ref[...]

    @pl.when(i >= nb_ref[0])
    def _():
        o_ref[...] = jnp.zeros(o_ref.shape, o_ref.dtype)


def _down(act, w_dn, b_dn, block_e, first, n_blocks, cfg):
    p, f = act.shape
    e, d, rb, tn = cfg.n_experts, cfg.d_model, cfg.moe_rows, cfg.dn_tn
    grid_spec = pltpu.PrefetchScalarGridSpec(
        num_scalar_prefetch=3, grid=(d // tn, p // rb),
        in_specs=[pl.BlockSpec((rb, f), lambda j, i, be, fi, nb: (i, 0)),
                  pl.BlockSpec((None, f, tn), lambda j, i, be, fi, nb: (be[i], 0, j)),
                  pl.BlockSpec((None, 1, tn), lambda j, i, be, fi, nb: (be[i], 0, j))],
        out_specs=pl.BlockSpec((rb, tn), lambda j, i, be, fi, nb: (i, j)),
        scratch_shapes=[pltpu.VMEM((f, tn), BF16)])
    return pl.pallas_call(
        _down_kernel, grid_spec=grid_spec,
        out_shape=jax.ShapeDtypeStruct((p, d), F32),
        compiler_params=_params(("arbitrary", "arbitrary")), name="moe_down",
    )(block_e, first, n_blocks, act, w_dn, b_dn.reshape(e, 1, d))


def _combine_kernel(pos_ref, ys_hbm, h_ref, gate_ref, g2_ref, b2_ref, o_ref, buf, sem, *, alpha):
    top_k, n = buf.shape[0], buf.shape[1]
    for k in range(top_k):
        def issue(t, c, k=k):
            _row_copy(ys_hbm, buf.at[k], sem, pos_ref[k, t], t).start()
            return c
        lax.fori_loop(0, n, issue, 0)
    for k in range(top_k):
        def drain(t, c, k=k):
            _row_copy(ys_hbm, buf.at[k], sem, 0, t).wait()
            return c
        lax.fori_loop(0, n, drain, 0)
    gates = gate_ref[...]
    y = gates[:, 0:1] * buf[0]
    for k in range(1, top_k):
        y = y + gates[:, k:k + 1] * buf[k]
    o_ref[...] = _layer_norm_rows(alpha * h_ref[...] + y, g2_ref[...], b2_ref[...])


def _combine(ys, pos, gates_tk, h1, g2, b2, cfg):
    t, d = h1.shape
    k, n = cfg.top_k, cfg.combine_rows
    alpha = (2 * cfg.depth) ** 0.25
    pos_blocks = pos.reshape(k, t // n, n).transpose(1, 0, 2)
    row = pl.BlockSpec((n, d), lambda i: (i, 0))
    vec = pl.BlockSpec((1, d), lambda i: (0, 0))
    return pl.pallas_call(
        functools.partial(_combine_kernel, alpha=alpha), grid=(t // n,),
        in_specs=[pl.BlockSpec((None, k, n), lambda i: (i, 0, 0), memory_space=pltpu.SMEM),
                  pl.BlockSpec(memory_space=pl.ANY), row,
                  pl.BlockSpec((n, k), lambda i: (i, 0)), vec, vec],
        out_specs=row,
        out_shape=jax.ShapeDtypeStruct((t, d), F32),
        scratch_shapes=[pltpu.VMEM((k, n, d), F32), pltpu.SemaphoreType.DMA],
        compiler_params=_params(("arbitrary",)), name="moe_combine_ln2",
    )(pos_blocks, ys, h1, gates_tk, g2.reshape(1, d), b2.reshape(1, d))


def _forward(cfg, x, ln_in_g, ln_in_b, w_in, conv_w, w_out, ln1_g, ln1_b, router_w, router_b,
             w_gate_up, b_gate_up, w_down, b_down, ln2_g, ln2_b):
    assert cfg.depth == 1
    b, s, d = x.shape
    x2d = x.reshape(b * s, d)
    slopes = jnp.asarray(_alibi_slopes(cfg.n_heads))
    h0 = _ln_in(x2d, ln_in_g, ln_in_b, cfg)
    proj = _matmul(h0, w_in[0], cfg, F32, "in_proj")
    attn = _attention(proj, slopes, cfg)
    y_conv = _gated_conv(proj, conv_w[0], cfg)
    mixed = jnp.concatenate([attn, y_conv], axis=-1)
    mix = _matmul(mixed, w_out[0], cfg, F32, "out_proj")
    h1, top_idx, gates = _ln1_router(x2d, mix, ln_in_g, ln_in_b, ln1_g[0], ln1_b[0],
                                     router_w[0], router_b[0], cfg)
    row_tok, pos, block_e, first, n_blocks = _routing_plan(top_idx, cfg)
    xs = _gather_rows(h1, row_tok, cfg)
    act = _gate_up(xs, w_gate_up[0], b_gate_up[0], block_e, first, n_blocks, cfg)
    ys = _down(act, w_down[0], b_down[0], block_e, first, n_blocks, cfg)
    out = _combine(ys, pos, gates.T, h1, ln2_g[0], ln2_b[0], cfg)
    return out.reshape(b, s, d)


_CONFIG = Config(
    batch=4, seq=2048, d_model=4096, head_dim=128, n_heads=24, conv_width=1024,
    patterns=((128, 1), (512, 4), (2048, 16)), n_experts=32, top_k=4, d_expert=2048, depth=1,
    ln_rows=256, mm_tm=1024, mm_tn=512, conv_cw=256, moe_rows=256, gu_tf=512, dn_tn=1024,
    gather_rows=256, combine_rows=64)


def kernel(x, ln_in_g, ln_in_b, w_in, conv_w, w_out, ln1_g, ln1_b, router_w, router_b,
           w_gate_up, b_gate_up, w_down, b_down, ln2_g, ln2_b):
    return _forward(_CONFIG, x, ln_in_g, ln_in_b, w_in, conv_w, w_out, ln1_g, ln1_b, router_w, router_b,
                    w_gate_up, b_gate_up, w_down, b_down, ln2_g, ln2_b)
```

```python
import dataclasses
import functools
import math

import numpy as np
import jax
import jax.numpy as jnp
from jax import lax
from jax.experimental import pallas as pl
from jax.experimental.pallas import tpu as pltpu

F32 = jnp.float32
BF16 = jnp.bfloat16

LN_EPS = 1e-5
SWIGLU_LIMIT = 7.0
SWIGLU_ALPHA = 1.702
MASKED_SCORE = -1e30
V7X_VMEM_LIMIT_BYTES = 56 * 1024 * 1024


@dataclasses.dataclass(frozen=True)
class Config:
    batch: int
    seq: int
    d_model: int
    head_dim: int
    n_heads: int
    conv_width: int
    patterns: tuple
    n_experts: int
    top_k: int
    d_expert: int
    depth: int
    ln_rows: int
    mm_tm: int
    mm_tn: int
    conv_cw: int
    moe_rows: int
    gu_tf: int
    dn_tn: int
    gather_rows: int
    combine_rows: int

    @property
    def attn_width(self):
        return self.n_heads * self.head_dim

    @property
    def tokens(self):
        return self.batch * self.seq


def _params(sem, vmem=V7X_VMEM_LIMIT_BYTES):
    return pltpu.CompilerParams(dimension_semantics=sem, vmem_limit_bytes=vmem)


def _alibi_slopes(n):
    def pow2(m):
        start = 2.0 ** (-(2.0 ** -(math.log2(m) - 3)))
        return [start ** (i + 1) for i in range(m)]
    if math.log2(n).is_integer():
        s = pow2(n)
    else:
        c = 2 ** int(math.floor(math.log2(n)))
        s = pow2(c) + pow2(2 * c)[0::2][: n - c]
    return np.asarray(s, dtype=np.float32)


def _layer_norm_rows(x, g, b):
    mu = jnp.mean(x, axis=-1, keepdims=True)
    xc = x - mu
    var = jnp.mean(xc * xc, axis=-1, keepdims=True)
    return xc * lax.rsqrt(var + LN_EPS) * g + b


def _ln_in_kernel(x_ref, g_ref, b_ref, o_ref):
    o_ref[...] = _layer_norm_rows(x_ref[...], g_ref[...], b_ref[...]).astype(o_ref.dtype)


def _ln_in(x2d, g, b, cfg):
    t, d = x2d.shape
    tm = cfg.ln_rows
    row = pl.BlockSpec((tm, d), lambda i: (i, 0))
    vec = pl.BlockSpec((1, d), lambda i: (0, 0))
    return pl.pallas_call(
        _ln_in_kernel, grid=(t // tm,), in_specs=[row, vec, vec], out_specs=row,
        out_shape=jax.ShapeDtypeStruct((t, d), BF16),
        compiler_params=_params(("parallel",)), name="ln_in",
    )(x2d, g.reshape(1, d), b.reshape(1, d))


def _matmul_kernel(x_ref, w_ref, o_ref):
    w = w_ref[...].astype(BF16)
    o_ref[...] = jnp.dot(x_ref[...], w, preferred_element_type=F32).astype(o_ref.dtype)


def _matmul(x, w, cfg, out_dtype, name):
    m, k = x.shape
    n = w.shape[1]
    tm, tn = cfg.mm_tm, cfg.mm_tn
    return pl.pallas_call(
        _matmul_kernel, grid=(m // tm, n // tn),
        in_specs=[pl.BlockSpec((tm, k), lambda i, j: (i, 0)),
                  pl.BlockSpec((k, tn), lambda i, j: (0, j))],
        out_specs=pl.BlockSpec((tm, tn), lambda i, j: (i, j)),
        out_shape=jax.ShapeDtypeStruct((m, n), out_dtype),
        compiler_params=_params(("parallel", "parallel")), name=name,
    )(x, w)


def _attn_kernel(slopes_ref, q_ref, k_ref, v_ref, o_ref, acc_ref, m_ref, l_ref, *, cfg, blk):
    seq, hd = cfg.seq, cfg.head_dim
    nblk = seq // blk
    slope = slopes_ref[pl.program_id(1)]
    scale = 1.0 / math.sqrt(hd)

    qi = lax.broadcasted_iota(jnp.int32, (blk, blk), 0)
    ki = lax.broadcasted_iota(jnp.int32, (blk, blk), 1)
    d_own = qi - ki
    own_steps = d_own.astype(F32)
    prev_steps = (d_own + blk).astype(F32)

    for g, (window, dil) in enumerate(cfg.patterns):
        sub = seq // dil
        nb = sub // blk
        neg = -slope * float(dil)

        def to_blocks(ref, f, dil=dil, sub=sub, nb=nb):
            if dil == 1:
                return f(ref[...]).reshape(nblk, blk, hd)
            return jnp.concatenate(
                [f(ref[pl.ds(r, sub, stride=dil), :]).reshape(nb, blk, hd) for r in range(dil)], axis=0)

        def from_blocks(dst, x, g=g, dil=dil, sub=sub, nb=nb):
            if dil == 1:
                dst[g] = x.reshape(seq, hd)
            else:
                for r in range(dil):
                    dst[g, pl.ds(r, sub, stride=dil), :] = x[r * nb:(r + 1) * nb].reshape(sub, hd)

        q = to_blocks(q_ref, lambda x: (x * scale).astype(BF16))
        keys = to_blocks(k_ref, lambda x: x.astype(BF16))
        vals = to_blocks(v_ref, lambda x: x.astype(BF16))
        bias = jnp.where(d_own >= 0, own_steps * neg, MASKED_SCORE)[None]
        if nb > 1:
            def with_prev(x, dil=dil, nb=nb):
                x4 = x.reshape(dil, nb, blk, hd)
                prev = jnp.concatenate([jnp.zeros((dil, 1, blk, hd), x.dtype), x4[:, :-1]], axis=1)
                return jnp.concatenate([prev.reshape(nblk, blk, hd), x], axis=1)
            keys, vals = with_prev(keys), with_prev(vals)
            n_of_block = lax.broadcasted_iota(jnp.int32, (nblk, blk, blk), 0) & (nb - 1)
            prev_ok = jnp.logical_and(d_own[None] <= 0, n_of_block > 0)
            bias_prev = jnp.where(prev_ok, (prev_steps * neg)[None], MASKED_SCORE)
            bias = jnp.concatenate([bias_prev, jnp.broadcast_to(bias, (nblk, blk, blk))], axis=2)
        s = jnp.einsum("bqd,bkd->bqk", q, keys, preferred_element_type=F32) + bias
        m = jnp.max(s, axis=-1, keepdims=True)
        p = jnp.exp(s - m).astype(BF16)
        vals1 = jnp.concatenate([vals, jnp.ones(vals.shape, BF16)], axis=-1)
        pv = jnp.einsum("bqk,bkd->bqd", p, vals1, preferred_element_type=F32)
        from_blocks(acc_ref, pv[:, :, :hd])
        from_blocks(l_ref, pv[:, :, hd:])
        from_blocks(m_ref, jnp.broadcast_to(m, (nblk, blk, hd)))

    n_win = len(cfg.patterns)
    gs = max(c for c in range(1, 5) if nblk % c == 0)

    def merge(i, carry):
        for j in range(gs):
            r = pl.ds(pl.multiple_of((i * gs + j) * blk, blk), blk)
            ms = [m_ref[g, r, :] for g in range(n_win)]
            m_all = functools.reduce(jnp.maximum, ms)
            ws = [jnp.exp(m - m_all) for m in ms]
            num = functools.reduce(lambda a, b: a + b, [w * acc_ref[g, r, :] for g, w in enumerate(ws)])
            den = functools.reduce(lambda a, b: a + b, [w * l_ref[g, r, :] for g, w in enumerate(ws)])
            o_ref[r, :] = (num / den).astype(o_ref.dtype)
        return carry
    lax.fori_loop(0, nblk // gs, merge, 0)


def _attention(proj, slopes, cfg):
    seq, hd, nh = cfg.seq, cfg.head_dim, cfg.n_heads
    blks = {w // d for w, d in cfg.patterns}
    assert len(blks) == 1, "all windows must share one block length"
    blk = blks.pop()
    for w, d in cfg.patterns:
        nb = seq // (d * blk)
        assert d & (d - 1) == 0 and nb & (nb - 1) == 0 and nb * d * blk == seq
    grid_spec = pltpu.PrefetchScalarGridSpec(
        num_scalar_prefetch=1, grid=(cfg.batch, nh),
        in_specs=[pl.BlockSpec((seq, hd), lambda b, h, s: (b, h)),
                  pl.BlockSpec((seq, hd), lambda b, h, s: (b, nh + h)),
                  pl.BlockSpec((seq, hd), lambda b, h, s: (b, 2 * nh + h))],
        out_specs=pl.BlockSpec((seq, hd), lambda b, h, s: (b, h)),
        scratch_shapes=[pltpu.VMEM((len(cfg.patterns), seq, hd), F32)] * 3)
    return pl.pallas_call(
        functools.partial(_attn_kernel, cfg=cfg, blk=blk), grid_spec=grid_spec,
        out_shape=jax.ShapeDtypeStruct((cfg.tokens, cfg.attn_width), BF16),
        compiler_params=_params(("parallel", "parallel")), name="dilated_attention",
    )(slopes, proj, proj, proj)


def _conv_kernel(gb_ref, gc_ref, val_ref, w_ref, o_ref):
    u = gc_ref[...] * val_ref[...]
    row = lax.broadcasted_iota(jnp.int32, u.shape, 0)
    u1 = jnp.where(row >= 1, pltpu.roll(u, 1, 0), 0.0)
    u2 = jnp.where(row >= 2, pltpu.roll(u, 2, 0), 0.0)
    w = w_ref[...]
    z = w[0:1, :] * u2 + w[1:2, :] * u1 + w[2:3, :] * u
    o_ref[...] = (gb_ref[...] * z).astype(o_ref.dtype)


def _gated_conv(proj, conv_w, cfg):
    seq, cw, c = cfg.seq, cfg.conv_cw, cfg.conv_width
    base = 3 * cfg.attn_width // cw
    per = c // cw
    spec = lambda off: pl.BlockSpec((seq, cw), lambda b, j, off=off: (b, base + off * per + j))
    return pl.pallas_call(
        _conv_kernel, grid=(cfg.batch, per),
        in_specs=[spec(0), spec(1), spec(2), pl.BlockSpec((conv_w.shape[0], cw), lambda b, j: (0, j))],
        out_specs=pl.BlockSpec((seq, cw), lambda b, j: (b, j)),
        out_shape=jax.ShapeDtypeStruct((cfg.tokens, c), BF16),
        compiler_params=_params(("parallel", "parallel")), name="gated_conv",
    )(proj, proj, proj, conv_w)


def _ln1_router_kernel(x_ref, mix_ref, gi_ref, bi_ref, g1_ref, b1_ref, rw_ref, rb_ref,
                       h_ref, idx_ref, gate_ref, *, alpha, top_k):
    h0 = _layer_norm_rows(x_ref[...], gi_ref[...], bi_ref[...])
    h1 = _layer_norm_rows(alpha * h0 + mix_ref[...], g1_ref[...], b1_ref[...])
    h_ref[...] = h1
    logits = lax.dot_general(rw_ref[...], h1, (((1,), (1,)), ((), ())),
                             precision=lax.Precision.HIGHEST, preferred_element_type=F32)
    logits = logits + rb_ref[...]
    n_e = logits.shape[0]
    eidx = lax.broadcasted_iota(jnp.int32, logits.shape, 0)
    vals, idxs = [], []
    for _ in range(top_k):
        m = jnp.max(logits, axis=0, keepdims=True)
        i = jnp.min(jnp.where(logits == m, eidx, n_e), axis=0, keepdims=True)
        vals.append(m)
        idxs.append(i)
        logits = jnp.where(eidx == i, -jnp.inf, logits)
    ex = [jnp.exp(v - vals[0]) for v in vals]
    den = functools.reduce(lambda a, b: a + b, ex)
    idx_ref[...] = jnp.concatenate(idxs, axis=0)
    gate_ref[...] = jnp.concatenate([e / den for e in ex], axis=0)


def _ln1_router(x2d, mix, gi, bi, g1, b1, router_w, router_b, cfg):
    t, d = x2d.shape
    e, k = cfg.n_experts, cfg.top_k
    tm = cfg.ln_rows
    alpha = (2 * cfg.depth) ** 0.25
    row = pl.BlockSpec((tm, d), lambda i: (i, 0))
    vec = pl.BlockSpec((1, d), lambda i: (0, 0))
    sel = pl.BlockSpec((k, tm), lambda i: (0, i))
    return pl.pallas_call(
        functools.partial(_ln1_router_kernel, alpha=alpha, top_k=k), grid=(t // tm,),
        in_specs=[row, row, vec, vec, vec, vec,
                  pl.BlockSpec((e, d), lambda i: (0, 0)), pl.BlockSpec((e, 1), lambda i: (0, 0))],
        out_specs=[row, sel, sel],
        out_shape=[jax.ShapeDtypeStruct((t, d), F32), jax.ShapeDtypeStruct((k, t), jnp.int32),
                   jax.ShapeDtypeStruct((k, t), F32)],
        compiler_params=_params(("parallel",)), name="ln1_router",
    )(x2d, mix, gi.reshape(1, d), bi.reshape(1, d), g1.reshape(1, d), b1.reshape(1, d),
      router_w.T, router_b.reshape(e, 1))


def _routing_plan(top_idx, cfg):
    k, t = top_idx.shape
    e, rb = cfg.n_experts, cfg.moe_rows
    a = k * t
    nb = a // rb + e
    p = nb * rb
    i32 = jnp.int32
    flat_e = top_idx.reshape(a)
    slot = jnp.arange(a, dtype=i32)
    experts = jnp.arange(e, dtype=i32)
    sorted_e, order = lax.sort((flat_e, slot), num_keys=1)
    starts = jnp.sum((flat_e[:, None] < experts[None, :]).astype(i32), axis=0)
    counts = jnp.sum((flat_e[:, None] == experts[None, :]).astype(i32), axis=0)
    padded = (counts + rb - 1) // rb * rb
    pends = jnp.cumsum(padded)
    pstarts = pends - padded
    pos_sorted = pstarts[sorted_e] + slot - starts[sorted_e]
    _, pos = lax.sort((order, pos_sorted), num_keys=1)

    rows = jnp.arange(p, dtype=i32)
    row_e = jnp.minimum(jnp.sum((rows[:, None] >= pends[None, :]).astype(i32), axis=1), e - 1)
    local = rows - pstarts[row_e]
    src = jnp.clip(starts[row_e] + local, 0, a - 1)
    row_tok = jnp.where(local < counts[row_e], order[src] % t, 0).astype(i32)

    n_blocks = pends[-1] // rb
    bi = jnp.arange(nb, dtype=i32)
    raw_e = row_e[::rb]
    block_e = jnp.where(bi < n_blocks, raw_e, raw_e[n_blocks - 1])
    first = jnp.concatenate([jnp.ones((1,), i32), (block_e[1:] != block_e[:-1]).astype(i32)])
    tile_ord = jnp.cumsum(first) - 1
    tile_e = jnp.max(jnp.where(tile_ord[None, :] == bi[:, None], block_e[None, :], -1), axis=1)
    next_e = tile_e[jnp.minimum(tile_ord + 1, nb - 1)]
    meta = jnp.stack([n_blocks, tile_ord[-1] + 1]).astype(i32)
    return (row_tok, pos.reshape(k, t).astype(i32), block_e.astype(i32), first, tile_ord.astype(i32),
            next_e.astype(i32), meta)


ROW_DMA_UNROLL = 8


def _row_copy(src_hbm, dst_vmem, sem, src_row, dst_row):
    return pltpu.make_async_copy(src_hbm.at[pl.ds(src_row, 1), :], dst_vmem.at[pl.ds(dst_row, 1), :], sem)


def _start_rows(src_hbm, dst_vmem, sem, n, src_row_of):
    def body(c, carry):
        for u in range(ROW_DMA_UNROLL):
            r = c * ROW_DMA_UNROLL + u
            _row_copy(src_hbm, dst_vmem, sem, src_row_of(r), r).start(priority=u % 2)
        return carry
    lax.fori_loop(0, n // ROW_DMA_UNROLL, body, 0)


def _wait_rows(src_hbm, dst_vmem, sem, n):
    def body(c, carry):
        for u in range(ROW_DMA_UNROLL):
            _row_copy(src_hbm, dst_vmem, sem, 0, c * ROW_DMA_UNROLL + u).wait()
        return carry
    lax.fori_loop(0, n // ROW_DMA_UNROLL, body, 0)


def _gather_kernel(tok_ref, tok_next_ref, h_hbm, o_ref, buf, sem):
    i = pl.program_id(0)
    n = buf.shape[1]
    slot = i % 2

    @pl.when(i == 0)
    def _():
        _start_rows(h_hbm, buf.at[0], sem.at[0], n, lambda r: tok_ref[0, r])

    @pl.when(i + 1 < pl.num_programs(0))
    def _():
        _start_rows(h_hbm, buf.at[1 - slot], sem.at[1 - slot], n, lambda r: tok_next_ref[0, r])

    _wait_rows(h_hbm, buf.at[slot], sem.at[slot], n)
    o_ref[...] = buf[slot].astype(o_ref.dtype)


def _gather_rows(h1, row_tok, cfg):
    p = row_tok.shape[0]
    d = h1.shape[1]
    r = cfg.gather_rows
    steps = p // r
    assert r % ROW_DMA_UNROLL == 0
    idx = lambda f: pl.BlockSpec((None, 1, r), lambda i: (f(i), 0, 0), memory_space=pltpu.SMEM)
    tok = row_tok.reshape(steps, 1, r)
    return pl.pallas_call(
        _gather_kernel, grid=(steps,),
        in_specs=[idx(lambda i: i), idx(lambda i: jnp.minimum(i + 1, steps - 1)),
                  pl.BlockSpec(memory_space=pl.ANY)],
        out_specs=pl.BlockSpec((r, d), lambda i: (i, 0)),
        out_shape=jax.ShapeDtypeStruct((p, d), BF16),
        scratch_shapes=[pltpu.VMEM((2, r, d), F32), pltpu.SemaphoreType.DMA((2,))],
        compiler_params=_params(("arbitrary",)), name="moe_gather",
    )(tok, tok, h1)


WEIGHT_CAST_CHUNKS = 4


class _TilePlan:
    def __init__(self, be_ref, first_ref, ord_ref, next_ref, meta_ref):
        j, i = pl.program_id(0), pl.program_id(1)
        self.j, self.i = j, i
        self.expert = be_ref[i]
        self.first = first_ref[i] == 1
        self.valid = i < meta_ref[0]
        tile = j * meta_ref[1] + ord_ref[i]
        self.slot = tile & 1
        self.is_tile0 = tile == 0
        nxt = next_ref[i]
        more_j = j + 1 < pl.num_programs(0)
        self.has_next = jnp.logical_or(nxt >= 0, more_j)
        self.next_e = jnp.where(nxt >= 0, nxt, be_ref[0])
        self.next_j = jnp.where(nxt >= 0, j, j + 1)


def _first_step_dot(x, stage, w_s):
    k = x.shape[1]
    ck = k // WEIGHT_CAST_CHUNKS
    acc = None
    for c in range(WEIGHT_CAST_CHUNKS):
        wb = stage[pl.ds(c * ck, ck), :].astype(BF16)
        w_s[pl.ds(c * ck, ck), :] = wb
        part = jnp.dot(x[:, c * ck:(c + 1) * ck], wb, preferred_element_type=F32)
        acc = part if acc is None else acc + part
    return acc


def _swiglu(g, u):
    g = jnp.minimum(g, SWIGLU_LIMIT)
    u = jnp.clip(u, -SWIGLU_LIMIT, SWIGLU_LIMIT)
    return (u + 1.0) * (g * jax.nn.sigmoid(SWIGLU_ALPHA * g))


def _gate_up_kernel(be_ref, first_ref, ord_ref, next_ref, meta_ref, x_ref, w_hbm, bg_ref, bu_ref, o_ref,
                    stage_g, stage_u, wg_s, wu_s, sem, *, tf, f):
    plan = _TilePlan(be_ref, first_ref, ord_ref, next_ref, meta_ref)

    def tile_copies(e, j, slot):
        col = pl.multiple_of(j * tf, tf)
        return (pltpu.make_async_copy(w_hbm.at[e, :, pl.ds(col, tf)], stage_g.at[slot], sem.at[slot]),
                pltpu.make_async_copy(w_hbm.at[e, :, pl.ds(f + col, tf)], stage_u.at[slot], sem.at[slot]))

    @pl.when(plan.first)
    def _():
        @pl.when(plan.is_tile0)
        def _():
            for c in tile_copies(plan.expert, plan.j, plan.slot):
                c.start()
        for c in tile_copies(plan.expert, plan.j, plan.slot):
            c.wait()

        @pl.when(plan.has_next)
        def _():
            for c in tile_copies(plan.next_e, plan.next_j, 1 - plan.slot):
                c.start()
        x = x_ref[...]
        g = _first_step_dot(x, stage_g.at[plan.slot], wg_s) + bg_ref[...]
        u = _first_step_dot(x, stage_u.at[plan.slot], wu_s) + bu_ref[...]
        o_ref[...] = _swiglu(g, u).astype(o_ref.dtype)

    @pl.when(jnp.logical_and(jnp.logical_not(plan.first), plan.valid))
    def _():
        x = x_ref[...]
        g = jnp.dot(x, wg_s[...], preferred_element_type=F32) + bg_ref[...]
        u = jnp.dot(x, wu_s[...], preferred_element_type=F32) + bu_ref[...]
        o_ref[...] = _swiglu(g, u).astype(o_ref.dtype)

    @pl.when(jnp.logical_not(plan.valid))
    def _():
        o_ref[...] = jnp.zeros(o_ref.shape, o_ref.dtype)


def _gate_up(xs, w_gu, b_gu, plan_arrays, cfg):
    p, d = xs.shape
    e, f, rb, tf = cfg.n_experts, cfg.d_expert, cfg.moe_rows, cfg.gu_tf
    nj = f // tf
    assert d % (WEIGHT_CAST_CHUNKS * 128) == 0
    grid_spec = pltpu.PrefetchScalarGridSpec(
        num_scalar_prefetch=5, grid=(nj, p // rb),
        in_specs=[pl.BlockSpec((rb, d), lambda j, i, be, *_: (i, 0)),
                  pl.BlockSpec(memory_space=pl.ANY),
                  pl.BlockSpec((None, 1, tf), lambda j, i, be, *_: (be[i], 0, j)),
                  pl.BlockSpec((None, 1, tf), lambda j, i, be, *_: (be[i], 0, nj + j))],
        out_specs=pl.BlockSpec((rb, tf), lambda j, i, be, *_: (i, j)),
        scratch_shapes=[pltpu.VMEM((2, d, tf), F32), pltpu.VMEM((2, d, tf), F32),
                        pltpu.VMEM((d, tf), BF16), pltpu.VMEM((d, tf), BF16),
                        pltpu.SemaphoreType.DMA((2,))])
    b3 = b_gu.reshape(e, 1, 2 * f)
    return pl.pallas_call(
        functools.partial(_gate_up_kernel, tf=tf, f=f), grid_spec=grid_spec,
        out_shape=jax.ShapeDtypeStruct((p, f), BF16),
        compiler_params=_params(("arbitrary", "arbitrary")), name="moe_gate_up",
    )(*plan_arrays, xs, w_gu, b3, b3)


def _down_kernel(be_ref, first_ref, ord_ref, next_ref, meta_ref, a_ref, w_hbm, b_ref, o_ref,
                 stage, w_s, sem, *, tn):
    plan = _TilePlan(be_ref, first_ref, ord_ref, next_ref, meta_ref)

    def tile_copy(e, j, slot):
        col = pl.multiple_of(j * tn, tn)
        return pltpu.make_async_copy(w_hbm.at[e, :, pl.ds(col, tn)], stage.at[slot], sem.at[slot])

    @pl.when(plan.first)
    def _():
        @pl.when(plan.is_tile0)
        def _():
            tile_copy(plan.expert, plan.j, plan.slot).start()
        tile_copy(plan.expert, plan.j, plan.slot).wait()

        @pl.when(plan.has_next)
        def _():
            tile_copy(plan.next_e, plan.next_j, 1 - plan.slot).start()
        o_ref[...] = _first_step_dot(a_ref[...], stage.at[plan.slot], w_s) + b_ref[...]

    @pl.when(jnp.logical_and(jnp.logical_not(plan.first), plan.valid))
    def _():
        o_ref[...] = jnp.dot(a_ref[...], w_s[...], preferred_element_type=F32) + b_ref[...]

    @pl.when(jnp.logical_not(plan.valid))
    def _():
        o_ref[...] = jnp.zeros(o_ref.shape, o_ref.dtype)


def _down(act, w_dn, b_dn, plan_arrays, cfg):
    p, f = act.shape
    e, d, rb, tn = cfg.n_experts, cfg.d_model, cfg.moe_rows, cfg.dn_tn
    assert f % (WEIGHT_CAST_CHUNKS * 128) == 0
    grid_spec = pltpu.PrefetchScalarGridSpec(
        num_scalar_prefetch=5, grid=(d // tn, p // rb),
        in_specs=[pl.BlockSpec((rb, f), lambda j, i, be, *_: (i, 0)),
                  pl.BlockSpec(memory_space=pl.ANY),
                  pl.BlockSpec((None, 1, tn), lambda j, i, be, *_: (be[i], 0, j))],
        out_specs=pl.BlockSpec((rb, tn), lambda j, i, be, *_: (i, j)),
        scratch_shapes=[pltpu.VMEM((2, f, tn), F32), pltpu.VMEM((f, tn), BF16),
                        pltpu.SemaphoreType.DMA((2,))])
    return pl.pallas_call(
        functools.partial(_down_kernel, tn=tn), grid_spec=grid_spec,
        out_shape=jax.ShapeDtypeStruct((p, d), F32),
        compiler_params=_params(("arbitrary", "arbitrary")), name="moe_down",
    )(*plan_arrays, act, w_dn, b_dn.reshape(e, 1, d))


def _combine_kernel(pos_ref, pos_next_ref, ys_hbm, h_ref, gate_ref, g2_ref, b2_ref, o_ref, buf, sem, *, alpha):
    i = pl.program_id(0)
    top_k, n = buf.shape[1], buf.shape[2]
    slot = i % 2

    def start(idx_ref, s):
        for k in range(top_k):
            _start_rows(ys_hbm, buf.at[s, k], sem.at[s], n, lambda t, k=k: idx_ref[k, t])

    @pl.when(i == 0)
    def _():
        start(pos_ref, 0)

    @pl.when(i + 1 < pl.num_programs(0))
    def _():
        start(pos_next_ref, 1 - slot)

    for k in range(top_k):
        _wait_rows(ys_hbm, buf.at[slot, k], sem.at[slot], n)
    gates = gate_ref[...]
    y = gates[:, 0:1] * buf[slot, 0]
    for k in range(1, top_k):
        y = y + gates[:, k:k + 1] * buf[slot, k]
    o_ref[...] = _layer_norm_rows(alpha * h_ref[...] + y, g2_ref[...], b2_ref[...])


def _combine(ys, pos, gates_tk, h1, g2, b2, cfg):
    t, d = h1.shape
    k, n = cfg.top_k, cfg.combine_rows
    steps = t // n
    assert n % ROW_DMA_UNROLL == 0
    alpha = (2 * cfg.depth) ** 0.25
    pos_blocks = pos.reshape(k, steps, n).transpose(1, 0, 2)
    idx = lambda f: pl.BlockSpec((None, k, n), lambda i: (f(i), 0, 0), memory_space=pltpu.SMEM)
    row = pl.BlockSpec((n, d), lambda i: (i, 0))
    vec = pl.BlockSpec((1, d), lambda i: (0, 0))
    return pl.pallas_call(
        functools.partial(_combine_kernel, alpha=alpha), grid=(steps,),
        in_specs=[idx(lambda i: i), idx(lambda i: jnp.minimum(i + 1, steps - 1)),
                  pl.BlockSpec(memory_space=pl.ANY), row,
                  pl.BlockSpec((n, k), lambda i: (i, 0)), vec, vec],
        out_specs=row,
        out_shape=jax.ShapeDtypeStruct((t, d), F32),
        scratch_shapes=[pltpu.VMEM((2, k, n, d), F32), pltpu.SemaphoreType.DMA((2,))],
        compiler_params=_params(("arbitrary",)), name="moe_combine_ln2",
    )(pos_blocks, pos_blocks, ys, h1, gates_tk, g2.reshape(1, d), b2.reshape(1, d))


def _forward(cfg, x, ln_in_g, ln_in_b, w_in, conv_w, w_out, ln1_g, ln1_b, router_w, router_b,
             w_gate_up, b_gate_up, w_down, b_down, ln2_g, ln2_b):
    assert cfg.depth == 1
    b, s, d = x.shape
    x2d = x.reshape(b * s, d)
    slopes = jnp.asarray(_alibi_slopes(cfg.n_heads))
    h0 = _ln_in(x2d, ln_in_g, ln_in_b, cfg)
    proj = _matmul(h0, w_in[0], cfg, F32, "in_proj")
    attn = _attention(proj, slopes, cfg)
    y_conv = _gated_conv(proj, conv_w[0], cfg)
    mixed = jnp.concatenate([attn, y_conv], axis=-1)
    mix = _matmul(mixed, w_out[0], cfg, F32, "out_proj")
    h1, top_idx, gates = _ln1_router(x2d, mix, ln_in_g, ln_in_b, ln1_g[0], ln1_b[0],
                                     router_w[0], router_b[0], cfg)
    row_tok, pos, *plan_arrays = _routing_plan(top_idx, cfg)
    xs = _gather_rows(h1, row_tok, cfg)
    act = _gate_up(xs, w_gate_up[0], b_gate_up[0], plan_arrays, cfg)
    ys = _down(act, w_down[0], b_down[0], plan_arrays, cfg)
    out = _combine(ys, pos, gates.T, h1, ln2_g[0], ln2_b[0], cfg)
    return out.reshape(b, s, d)


_CONFIG = Config(
    batch=4, seq=2048, d_model=4096, head_dim=128, n_heads=24, conv_width=1024,
    patterns=((128, 1), (512, 4), (2048, 16)), n_experts=32, top_k=4, d_expert=2048, depth=1,
    ln_rows=256, mm_tm=1024, mm_tn=512, conv_cw=256, moe_rows=256, gu_tf=512, dn_tn=1024,
    gather_rows=256, combine_rows=64)


def kernel(x, ln_in_g, ln_in_b, w_in, conv_w, w_out, ln1_g, ln1_b, router_w, router_b,
           w_gate_up, b_gate_up, w_down, b_down, ln2_g, ln2_b):
    return _forward(_CONFIG, x, ln_in_g, ln_in_b, w_in, conv_w, w_out, ln1_g, ln1_b, router_w, router_b,
                    w_gate_up, b_gate_up, w_down, b_down, ln2_g, ln2_b)
```

```python
import dataclasses
import functools
import math

import numpy as np
import jax
import jax.numpy as jnp
from jax import lax
from jax.experimental import pallas as pl
from jax.experimental.pallas import tpu as pltpu

F32 = jnp.float32
BF16 = jnp.bfloat16

LN_EPS = 1e-5
SWIGLU_LIMIT = 7.0
SWIGLU_ALPHA = 1.702
MASKED_SCORE = -1e30
V7X_VMEM_LIMIT_BYTES = 56 * 1024 * 1024


@dataclasses.dataclass(frozen=True)
class Config:
    batch: int
    seq: int
    d_model: int
    head_dim: int
    n_heads: int
    conv_width: int
    patterns: tuple
    n_experts: int
    top_k: int
    d_expert: int
    depth: int
    ln_rows: int
    mm_tm: int
    mm_tn: int
    conv_cw: int
    moe_rows: int
    gu_tf: int
    dn_tn: int
    gather_rows: int
    combine_rows: int

    @property
    def attn_width(self):
        return self.n_heads * self.head_dim

    @property
    def tokens(self):
        return self.batch * self.seq


def _params(sem, vmem=V7X_VMEM_LIMIT_BYTES):
    return pltpu.CompilerParams(dimension_semantics=sem, vmem_limit_bytes=vmem)


def _alibi_slopes(n):
    def pow2(m):
        start = 2.0 ** (-(2.0 ** -(math.log2(m) - 3)))
        return [start ** (i + 1) for i in range(m)]
    if math.log2(n).is_integer():
        s = pow2(n)
    else:
        c = 2 ** int(math.floor(math.log2(n)))
        s = pow2(c) + pow2(2 * c)[0::2][: n - c]
    return np.asarray(s, dtype=np.float32)


def _layer_norm_rows(x, g, b):
    mu = jnp.mean(x, axis=-1, keepdims=True)
    xc = x - mu
    var = jnp.mean(xc * xc, axis=-1, keepdims=True)
    return xc * lax.rsqrt(var + LN_EPS) * g + b


def _ln_in_kernel(x_ref, g_ref, b_ref, o_ref):
    o_ref[...] = _layer_norm_rows(x_ref[...], g_ref[...], b_ref[...]).astype(o_ref.dtype)


def _ln_in(x2d, g, b, cfg):
    t, d = x2d.shape
    tm = cfg.ln_rows
    row = pl.BlockSpec((tm, d), lambda i: (i, 0))
    vec = pl.BlockSpec((1, d), lambda i: (0, 0))
    return pl.pallas_call(
        _ln_in_kernel, grid=(t // tm,), in_specs=[row, vec, vec], out_specs=row,
        out_shape=jax.ShapeDtypeStruct((t, d), BF16),
        compiler_params=_params(("parallel",)), name="ln_in",
    )(x2d, g.reshape(1, d), b.reshape(1, d))


def _matmul_kernel(*refs):
    *xw_refs, o_ref = refs
    n_in = len(xw_refs) // 2
    acc = None
    for x_ref, w_ref in zip(xw_refs[:n_in], xw_refs[n_in:]):
        part = jnp.dot(x_ref[...], w_ref[...].astype(BF16), preferred_element_type=F32)
        acc = part if acc is None else acc + part
    o_ref[...] = acc.astype(o_ref.dtype)


def _matmul(xs, w, cfg, out_dtype, name):
    m = xs[0].shape[0]
    n = w.shape[1]
    tm, tn = cfg.mm_tm, cfg.mm_tn
    x_specs, w_specs, row0 = [], [], 0
    for x in xs:
        k = x.shape[1]
        assert row0 % k == 0
        x_specs.append(pl.BlockSpec((tm, k), lambda i, j: (i, 0)))
        w_specs.append(pl.BlockSpec((k, tn), lambda i, j, r=row0 // k: (r, j)))
        row0 += k
    assert row0 == w.shape[0]
    return pl.pallas_call(
        _matmul_kernel, grid=(m // tm, n // tn),
        in_specs=x_specs + w_specs,
        out_specs=pl.BlockSpec((tm, tn), lambda i, j: (i, j)),
        out_shape=jax.ShapeDtypeStruct((m, n), out_dtype),
        compiler_params=_params(("parallel", "parallel")), name=name,
    )(*xs, *([w] * len(xs)))


def _attn_kernel(slopes_ref, q_ref, k_ref, v_ref, o_ref, acc_ref, m_ref, l_ref, *, cfg, blk):
    seq, hd = cfg.seq, cfg.head_dim
    nblk = seq // blk
    slope = slopes_ref[pl.program_id(1)]
    scale = 1.0 / math.sqrt(hd)

    qi = lax.broadcasted_iota(jnp.int32, (blk, blk), 0)
    ki = lax.broadcasted_iota(jnp.int32, (blk, blk), 1)
    d_own = qi - ki
    own_steps = d_own.astype(F32)
    prev_steps = (d_own + blk).astype(F32)

    for g, (window, dil) in enumerate(cfg.patterns):
        sub = seq // dil
        nb = sub // blk
        neg = -slope * float(dil)

        def to_blocks(ref, f, dil=dil, sub=sub, nb=nb):
            if dil == 1:
                return f(ref[...]).reshape(nblk, blk, hd)
            return jnp.concatenate(
                [f(ref[pl.ds(r, sub, stride=dil), :]).reshape(nb, blk, hd) for r in range(dil)], axis=0)

        def from_blocks(dst, x, g=g, dil=dil, sub=sub, nb=nb):
            if dil == 1:
                dst[g] = x.reshape(seq, hd)
            else:
                for r in range(dil):
                    dst[g, pl.ds(r, sub, stride=dil), :] = x[r * nb:(r + 1) * nb].reshape(sub, hd)

        q = to_blocks(q_ref, lambda x: (x * scale).astype(BF16))
        keys = to_blocks(k_ref, lambda x: x.astype(BF16))
        vals = to_blocks(v_ref, lambda x: x.astype(BF16))
        bias = jnp.where(d_own >= 0, own_steps * neg, MASKED_SCORE)[None]
        if nb > 1:
            def with_prev(x, dil=dil, nb=nb):
                x4 = x.reshape(dil, nb, blk, hd)
                prev = jnp.concatenate([jnp.zeros((dil, 1, blk, hd), x.dtype), x4[:, :-1]], axis=1)
                return jnp.concatenate([prev.reshape(nblk, blk, hd), x], axis=1)
            keys, vals = with_prev(keys), with_prev(vals)
            n_of_block = lax.broadcasted_iota(jnp.int32, (nblk, blk, blk), 0) & (nb - 1)
            prev_ok = jnp.logical_and(d_own[None] <= 0, n_of_block > 0)
            bias_prev = jnp.where(prev_ok, (prev_steps * neg)[None], MASKED_SCORE)
            bias = jnp.concatenate([bias_prev, jnp.broadcast_to(bias, (nblk, blk, blk))], axis=2)
        s = jnp.einsum("bqd,bkd->bqk", q, keys, preferred_element_type=F32) + bias
        m = jnp.max(s, axis=-1, keepdims=True)
        p = jnp.exp(s - m).astype(BF16)
        vals1 = jnp.concatenate([vals, jnp.ones(vals.shape, BF16)], axis=-1)
        pv = jnp.einsum("bqk,bkd->bqd", p, vals1, preferred_element_type=F32)
        from_blocks(acc_ref, pv[:, :, :hd])
        from_blocks(l_ref, pv[:, :, hd:])
        from_blocks(m_ref, jnp.broadcast_to(m, (nblk, blk, hd)))

    n_win = len(cfg.patterns)
    gs = max(c for c in range(1, 5) if nblk % c == 0)

    def merge(i, carry):
        for j in range(gs):
            r = pl.ds(pl.multiple_of((i * gs + j) * blk, blk), blk)
            ms = [m_ref[g, r, :] for g in range(n_win)]
            m_all = functools.reduce(jnp.maximum, ms)
            ws = [jnp.exp(m - m_all) for m in ms]
            num = functools.reduce(lambda a, b: a + b, [w * acc_ref[g, r, :] for g, w in enumerate(ws)])
            den = functools.reduce(lambda a, b: a + b, [w * l_ref[g, r, :] for g, w in enumerate(ws)])
            o_ref[r, :] = (num / den).astype(o_ref.dtype)
        return carry
    lax.fori_loop(0, nblk // gs, merge, 0)


def _attention(proj, slopes, cfg):
    seq, hd, nh = cfg.seq, cfg.head_dim, cfg.n_heads
    blks = {w // d for w, d in cfg.patterns}
    assert len(blks) == 1, "all windows must share one block length"
    blk = blks.pop()
    for w, d in cfg.patterns:
        nb = seq // (d * blk)
        assert d & (d - 1) == 0 and nb & (nb - 1) == 0 and nb * d * blk == seq
    grid_spec = pltpu.PrefetchScalarGridSpec(
        num_scalar_prefetch=1, grid=(cfg.batch, nh),
        in_specs=[pl.BlockSpec((seq, hd), lambda b, h, s: (b, h)),
                  pl.BlockSpec((seq, hd), lambda b, h, s: (b, nh + h)),
                  pl.BlockSpec((seq, hd), lambda b, h, s: (b, 2 * nh + h))],
        out_specs=pl.BlockSpec((seq, hd), lambda b, h, s: (b, h)),
        scratch_shapes=[pltpu.VMEM((len(cfg.patterns), seq, hd), F32)] * 3)
    return pl.pallas_call(
        functools.partial(_attn_kernel, cfg=cfg, blk=blk), grid_spec=grid_spec,
        out_shape=jax.ShapeDtypeStruct((cfg.tokens, cfg.attn_width), BF16),
        compiler_params=_params(("parallel", "parallel")), name="dilated_attention",
    )(slopes, proj, proj, proj)


def _conv_kernel(gb_ref, gc_ref, val_ref, w_ref, o_ref):
    u = gc_ref[...] * val_ref[...]
    row = lax.broadcasted_iota(jnp.int32, u.shape, 0)
    u1 = jnp.where(row >= 1, pltpu.roll(u, 1, 0), 0.0)
    u2 = jnp.where(row >= 2, pltpu.roll(u, 2, 0), 0.0)
    w = w_ref[...]
    z = w[0:1, :] * u2 + w[1:2, :] * u1 + w[2:3, :] * u
    o_ref[...] = (gb_ref[...] * z).astype(o_ref.dtype)


def _gated_conv(proj, conv_w, cfg):
    seq, cw, c = cfg.seq, cfg.conv_cw, cfg.conv_width
    base = 3 * cfg.attn_width // cw
    per = c // cw
    spec = lambda off: pl.BlockSpec((seq, cw), lambda b, j, off=off: (b, base + off * per + j))
    return pl.pallas_call(
        _conv_kernel, grid=(cfg.batch, per),
        in_specs=[spec(0), spec(1), spec(2), pl.BlockSpec((conv_w.shape[0], cw), lambda b, j: (0, j))],
        out_specs=pl.BlockSpec((seq, cw), lambda b, j: (b, j)),
        out_shape=jax.ShapeDtypeStruct((cfg.tokens, c), BF16),
        compiler_params=_params(("parallel", "parallel")), name="gated_conv",
    )(proj, proj, proj, conv_w)


def _ln1_router_kernel(x_ref, mix_ref, gi_ref, bi_ref, g1_ref, b1_ref, rw_ref, rb_ref,
                       h_ref, idx_ref, gate_ref, *, alpha, top_k):
    h0 = _layer_norm_rows(x_ref[...], gi_ref[...], bi_ref[...])
    h1 = _layer_norm_rows(alpha * h0 + mix_ref[...], g1_ref[...], b1_ref[...])
    h_ref[...] = h1
    def split(v):
        hi = v.astype(BF16)
        return hi, (v - hi.astype(F32)).astype(BF16)

    def dot_t(a, b):
        return lax.dot_general(a, b, (((1,), (1,)), ((), ())), preferred_element_type=F32)
    w_hi, w_lo = split(rw_ref[...])
    h_hi, h_lo = split(h1)
    logits = dot_t(w_hi, h_hi) + (dot_t(w_hi, h_lo) + dot_t(w_lo, h_hi)) + rb_ref[...]
    n_e = logits.shape[0]
    eidx = lax.broadcasted_iota(jnp.int32, logits.shape, 0)
    vals, idxs = [], []
    for _ in range(top_k):
        m = jnp.max(logits, axis=0, keepdims=True)
        i = jnp.min(jnp.where(logits == m, eidx, n_e), axis=0, keepdims=True)
        vals.append(m)
        idxs.append(i)
        logits = jnp.where(eidx == i, -jnp.inf, logits)
    ex = [jnp.exp(v - vals[0]) for v in vals]
    den = functools.reduce(lambda a, b: a + b, ex)
    idx_ref[...] = jnp.concatenate(idxs, axis=0)
    gate_ref[...] = jnp.concatenate([e / den for e in ex], axis=0)


def _ln1_router(x2d, mix, gi, bi, g1, b1, router_w, router_b, cfg):
    t, d = x2d.shape
    e, k = cfg.n_experts, cfg.top_k
    tm = cfg.ln_rows
    alpha = (2 * cfg.depth) ** 0.25
    row = pl.BlockSpec((tm, d), lambda i: (i, 0))
    vec = pl.BlockSpec((1, d), lambda i: (0, 0))
    sel = pl.BlockSpec((k, tm), lambda i: (0, i))
    return pl.pallas_call(
        functools.partial(_ln1_router_kernel, alpha=alpha, top_k=k), grid=(t // tm,),
        in_specs=[row, row, vec, vec, vec, vec,
                  pl.BlockSpec((e, d), lambda i: (0, 0)), pl.BlockSpec((e, 1), lambda i: (0, 0))],
        out_specs=[row, sel, sel],
        out_shape=[jax.ShapeDtypeStruct((t, d), F32), jax.ShapeDtypeStruct((k, t), jnp.int32),
                   jax.ShapeDtypeStruct((k, t), F32)],
        compiler_params=_params(("parallel",)), name="ln1_router",
    )(x2d, mix, gi.reshape(1, d), bi.reshape(1, d), g1.reshape(1, d), b1.reshape(1, d),
      router_w.T, router_b.reshape(e, 1))


def _routing_plan(top_idx, cfg):
    k, t = top_idx.shape
    e, rb = cfg.n_experts, cfg.moe_rows
    a = k * t
    nb = a // rb + e
    p = nb * rb
    i32 = jnp.int32
    flat_e = top_idx.reshape(a)
    slot = jnp.arange(a, dtype=i32)
    experts = jnp.arange(e, dtype=i32)
    sorted_e, order = lax.sort((flat_e, slot), num_keys=1)
    starts = jnp.sum((flat_e[:, None] < experts[None, :]).astype(i32), axis=0)
    counts = jnp.sum((flat_e[:, None] == experts[None, :]).astype(i32), axis=0)
    padded = (counts + rb - 1) // rb * rb
    pends = jnp.cumsum(padded)
    pstarts = pends - padded
    pos_sorted = pstarts[sorted_e] + slot - starts[sorted_e]
    _, pos = lax.sort((order, pos_sorted), num_keys=1)

    rows = jnp.arange(p, dtype=i32)
    row_e = jnp.minimum(jnp.sum((rows[:, None] >= pends[None, :]).astype(i32), axis=1), e - 1)
    local = rows - pstarts[row_e]
    src = jnp.clip(starts[row_e] + local, 0, a - 1)
    row_tok = jnp.where(local < counts[row_e], order[src] % t, 0).astype(i32)

    n_blocks = pends[-1] // rb
    bi = jnp.arange(nb, dtype=i32)
    raw_e = row_e[::rb]
    block_e = jnp.where(bi < n_blocks, raw_e, raw_e[n_blocks - 1])
    first = jnp.concatenate([jnp.ones((1,), i32), (block_e[1:] != block_e[:-1]).astype(i32)])
    tile_ord = jnp.cumsum(first) - 1
    tile_e = jnp.max(jnp.where(tile_ord[None, :] == bi[:, None], block_e[None, :], -1), axis=1)
    next_e = tile_e[jnp.minimum(tile_ord + 1, nb - 1)]
    meta = jnp.stack([n_blocks, tile_ord[-1] + 1]).astype(i32)
    return (row_tok, pos.reshape(k, t).astype(i32), block_e.astype(i32), first, tile_ord.astype(i32),
            next_e.astype(i32), meta)


ROW_DMA_UNROLL = 8


def _row_copy(src_hbm, dst_vmem, sem, src_row, dst_row):
    return pltpu.make_async_copy(src_hbm.at[pl.ds(src_row, 1), :], dst_vmem.at[pl.ds(dst_row, 1), :], sem)


def _start_rows(src_hbm, dst_vmem, sem, n, src_row_of):
    def body(c, carry):
        for u in range(ROW_DMA_UNROLL):
            r = c * ROW_DMA_UNROLL + u
            _row_copy(src_hbm, dst_vmem, sem, src_row_of(r), r).start(priority=u % 2)
        return carry
    lax.fori_loop(0, n // ROW_DMA_UNROLL, body, 0)


def _wait_rows(src_hbm, dst_vmem, sem, n):
    def body(c, carry):
        for u in range(ROW_DMA_UNROLL):
            _row_copy(src_hbm, dst_vmem, sem, 0, c * ROW_DMA_UNROLL + u).wait()
        return carry
    lax.fori_loop(0, n // ROW_DMA_UNROLL, body, 0)


def _gather_kernel(meta_ref, tok_ref, tok_next_ref, h_hbm, o_ref, buf, sem):
    i = pl.program_id(0)
    n = buf.shape[1]
    slot = i % 2
    live = meta_ref[0]

    @pl.when(i == 0)
    def _():
        _start_rows(h_hbm, buf.at[0], sem.at[0], n, lambda r: tok_ref[0, r])

    @pl.when(i + 1 < live)
    def _():
        _start_rows(h_hbm, buf.at[1 - slot], sem.at[1 - slot], n, lambda r: tok_next_ref[0, r])

    @pl.when(i < live)
    def _():
        _wait_rows(h_hbm, buf.at[slot], sem.at[slot], n)
        o_ref[...] = buf[slot].astype(o_ref.dtype)

    @pl.when(i >= live)
    def _():
        o_ref[...] = jnp.zeros(o_ref.shape, o_ref.dtype)


def _gather_rows(h1, row_tok, meta, cfg):
    p = row_tok.shape[0]
    d = h1.shape[1]
    r = cfg.gather_rows
    steps = p // r
    assert r % ROW_DMA_UNROLL == 0 and r == cfg.moe_rows
    idx = lambda f: pl.BlockSpec((None, 1, r), lambda i: (f(i), 0, 0), memory_space=pltpu.SMEM)
    tok = row_tok.reshape(steps, 1, r)
    return pl.pallas_call(
        _gather_kernel, grid=(steps,),
        in_specs=[pl.BlockSpec(memory_space=pltpu.SMEM),
                  idx(lambda i: i), idx(lambda i: jnp.minimum(i + 1, steps - 1)),
                  pl.BlockSpec(memory_space=pl.ANY)],
        out_specs=pl.BlockSpec((r, d), lambda i: (i, 0)),
        out_shape=jax.ShapeDtypeStruct((p, d), BF16),
        scratch_shapes=[pltpu.VMEM((2, r, d), F32), pltpu.SemaphoreType.DMA((2,))],
        compiler_params=_params(("arbitrary",)), name="moe_gather",
    )(meta, tok, tok, h1)


WEIGHT_CAST_CHUNKS = 4
WEIGHT_DMA_PRIORITY = 1


class _TilePlan:
    def __init__(self, be_ref, first_ref, ord_ref, next_ref, meta_ref):
        j, i = pl.program_id(0), pl.program_id(1)
        self.j, self.i = j, i
        self.expert = be_ref[i]
        self.first = first_ref[i] == 1
        self.valid = i < meta_ref[0]
        tile = j * meta_ref[1] + ord_ref[i]
        self.slot = tile & 1
        self.is_tile0 = tile == 0
        nxt = next_ref[i]
        more_j = j + 1 < pl.num_programs(0)
        self.has_next = jnp.logical_or(nxt >= 0, more_j)
        self.next_e = jnp.where(nxt >= 0, nxt, be_ref[0])
        self.next_j = jnp.where(nxt >= 0, j, j + 1)


def _first_step_dot(x, stage, w_s):
    k = x.shape[1]
    ck = k // WEIGHT_CAST_CHUNKS
    acc = None
    for c in range(WEIGHT_CAST_CHUNKS):
        wb = stage[pl.ds(c * ck, ck), :].astype(BF16)
        w_s[pl.ds(c * ck, ck), :] = wb
        part = jnp.dot(x[:, c * ck:(c + 1) * ck], wb, preferred_element_type=F32)
        acc = part if acc is None else acc + part
    return acc


def _swiglu(g, u):
    g = jnp.minimum(g, SWIGLU_LIMIT)
    u = jnp.clip(u, -SWIGLU_LIMIT, SWIGLU_LIMIT)
    return (u + 1.0) * (g * jax.nn.sigmoid(SWIGLU_ALPHA * g))


def _gate_up_kernel(be_ref, first_ref, ord_ref, next_ref, meta_ref, x_ref, w_hbm, bg_ref, bu_ref, o_ref,
                    stage_g, stage_u, wg_s, wu_s, sem, *, tf, f):
    plan = _TilePlan(be_ref, first_ref, ord_ref, next_ref, meta_ref)

    def tile_copies(e, j, slot):
        col = pl.multiple_of(j * tf, tf)
        return (pltpu.make_async_copy(w_hbm.at[e, :, pl.ds(col, tf)], stage_g.at[slot], sem.at[slot]),
                pltpu.make_async_copy(w_hbm.at[e, :, pl.ds(f + col, tf)], stage_u.at[slot], sem.at[slot]))

    @pl.when(plan.first)
    def _():
        @pl.when(plan.is_tile0)
        def _():
            for c in tile_copies(plan.expert, plan.j, plan.slot):
                c.start(priority=WEIGHT_DMA_PRIORITY)
        for c in tile_copies(plan.expert, plan.j, plan.slot):
            c.wait()

        @pl.when(plan.has_next)
        def _():
            for c in tile_copies(plan.next_e, plan.next_j, 1 - plan.slot):
                c.start(priority=WEIGHT_DMA_PRIORITY)
        x = x_ref[...]
        g = _first_step_dot(x, stage_g.at[plan.slot], wg_s) + bg_ref[...]
        u = _first_step_dot(x, stage_u.at[plan.slot], wu_s) + bu_ref[...]
        o_ref[...] = _swiglu(g, u).astype(o_ref.dtype)

    @pl.when(jnp.logical_and(jnp.logical_not(plan.first), plan.valid))
    def _():
        x = x_ref[...]
        g = jnp.dot(x, wg_s[...], preferred_element_type=F32) + bg_ref[...]
        u = jnp.dot(x, wu_s[...], preferred_element_type=F32) + bu_ref[...]
        o_ref[...] = _swiglu(g, u).astype(o_ref.dtype)

    @pl.when(jnp.logical_not(plan.valid))
    def _():
        o_ref[...] = jnp.zeros(o_ref.shape, o_ref.dtype)


def _gate_up(xs, w_gu, b_gu, plan_arrays, cfg):
    p, d = xs.shape
    e, f, rb, tf = cfg.n_experts, cfg.d_expert, cfg.moe_rows, cfg.gu_tf
    nj = f // tf
    assert d % (WEIGHT_CAST_CHUNKS * 128) == 0
    grid_spec = pltpu.PrefetchScalarGridSpec(
        num_scalar_prefetch=5, grid=(nj, p // rb),
        in_specs=[pl.BlockSpec((rb, d), lambda j, i, be, *_: (i, 0)),
                  pl.BlockSpec(memory_space=pl.ANY),
                  pl.BlockSpec((None, 1, tf), lambda j, i, be, *_: (be[i], 0, j)),
                  pl.BlockSpec((None, 1, tf), lambda j, i, be, *_: (be[i], 0, nj + j))],
        out_specs=pl.BlockSpec((rb, tf), lambda j, i, be, *_: (i, j)),
        scratch_shapes=[pltpu.VMEM((2, d, tf), F32), pltpu.VMEM((2, d, tf), F32),
                        pltpu.VMEM((d, tf), BF16), pltpu.VMEM((d, tf), BF16),
                        pltpu.SemaphoreType.DMA((2,))])
    b3 = b_gu.reshape(e, 1, 2 * f)
    return pl.pallas_call(
        functools.partial(_gate_up_kernel, tf=tf, f=f), grid_spec=grid_spec,
        out_shape=jax.ShapeDtypeStruct((p, f), BF16),
        compiler_params=_params(("arbitrary", "arbitrary")), name="moe_gate_up",
    )(*plan_arrays, xs, w_gu, b3, b3)


def _down_kernel(be_ref, first_ref, ord_ref, next_ref, meta_ref, a_ref, w_hbm, b_ref, o_ref,
                 stage, w_s, sem, *, tn):
    plan = _TilePlan(be_ref, first_ref, ord_ref, next_ref, meta_ref)

    def tile_copy(e, j, slot):
        col = pl.multiple_of(j * tn, tn)
        return pltpu.make_async_copy(w_hbm.at[e, :, pl.ds(col, tn)], stage.at[slot], sem.at[slot])

    @pl.when(plan.first)
    def _():
        @pl.when(plan.is_tile0)
        def _():
            tile_copy(plan.expert, plan.j, plan.slot).start(priority=WEIGHT_DMA_PRIORITY)
        tile_copy(plan.expert, plan.j, plan.slot).wait()

        @pl.when(plan.has_next)
        def _():
            tile_copy(plan.next_e, plan.next_j, 1 - plan.slot).start(priority=WEIGHT_DMA_PRIORITY)
        o_ref[...] = _first_step_dot(a_ref[...], stage.at[plan.slot], w_s) + b_ref[...]

    @pl.when(jnp.logical_and(jnp.logical_not(plan.first), plan.valid))
    def _():
        o_ref[...] = jnp.dot(a_ref[...], w_s[...], preferred_element_type=F32) + b_ref[...]

    @pl.when(jnp.logical_not(plan.valid))
    def _():
        o_ref[...] = jnp.zeros(o_ref.shape, o_ref.dtype)


def _down(act, w_dn, b_dn, plan_arrays, cfg):
    p, f = act.shape
    e, d, rb, tn = cfg.n_experts, cfg.d_model, cfg.moe_rows, cfg.dn_tn
    assert f % (WEIGHT_CAST_CHUNKS * 128) == 0
    grid_spec = pltpu.PrefetchScalarGridSpec(
        num_scalar_prefetch=5, grid=(d // tn, p // rb),
        in_specs=[pl.BlockSpec((rb, f), lambda j, i, be, *_: (i, 0)),
                  pl.BlockSpec(memory_space=pl.ANY),
                  pl.BlockSpec((None, 1, tn), lambda j, i, be, *_: (be[i], 0, j))],
        out_specs=pl.BlockSpec((rb, tn), lambda j, i, be, *_: (i, j)),
        scratch_shapes=[pltpu.VMEM((2, f, tn), F32), pltpu.VMEM((f, tn), BF16),
                        pltpu.SemaphoreType.DMA((2,))])
    return pl.pallas_call(
        functools.partial(_down_kernel, tn=tn), grid_spec=grid_spec,
        out_shape=jax.ShapeDtypeStruct((p, d), F32),
        compiler_params=_params(("arbitrary", "arbitrary")), name="moe_down",
    )(*plan_arrays, act, w_dn, b_dn.reshape(e, 1, d))


def _combine_kernel(pos_ref, pos_next_ref, ys_hbm, h_ref, gate_ref, g2_ref, b2_ref, o_ref, buf, sem, *, alpha):
    i = pl.program_id(0)
    top_k, n = buf.shape[1], buf.shape[2]
    slot = i % 2

    def start(idx_ref, s):
        for k in range(top_k):
            _start_rows(ys_hbm, buf.at[s, k], sem.at[s], n, lambda t, k=k: idx_ref[k, t])

    @pl.when(i == 0)
    def _():
        start(pos_ref, 0)

    @pl.when(i + 1 < pl.num_programs(0))
    def _():
        start(pos_next_ref, 1 - slot)

    for k in range(top_k):
        _wait_rows(ys_hbm, buf.at[slot, k], sem.at[slot], n)
    gates = gate_ref[...]
    y = gates[:, 0:1] * buf[slot, 0]
    for k in range(1, top_k):
        y = y + gates[:, k:k + 1] * buf[slot, k]
    o_ref[...] = _layer_norm_rows(alpha * h_ref[...] + y, g2_ref[...], b2_ref[...])


def _combine(ys, pos, gates_tk, h1, g2, b2, cfg):
    t, d = h1.shape
    k, n = cfg.top_k, cfg.combine_rows
    steps = t // n
    assert n % ROW_DMA_UNROLL == 0
    alpha = (2 * cfg.depth) ** 0.25
    pos_blocks = pos.reshape(k, steps, n).transpose(1, 0, 2)
    idx = lambda f: pl.BlockSpec((None, k, n), lambda i: (f(i), 0, 0), memory_space=pltpu.SMEM)
    row = pl.BlockSpec((n, d), lambda i: (i, 0))
    vec = pl.BlockSpec((1, d), lambda i: (0, 0))
    return pl.pallas_call(
        functools.partial(_combine_kernel, alpha=alpha), grid=(steps,),
        in_specs=[idx(lambda i: i), idx(lambda i: jnp.minimum(i + 1, steps - 1)),
                  pl.BlockSpec(memory_space=pl.ANY), row,
                  pl.BlockSpec((n, k), lambda i: (i, 0)), vec, vec],
        out_specs=row,
        out_shape=jax.ShapeDtypeStruct((t, d), F32),
        scratch_shapes=[pltpu.VMEM((2, k, n, d), F32), pltpu.SemaphoreType.DMA((2,))],
        compiler_params=_params(("arbitrary",)), name="moe_combine_ln2",
    )(pos_blocks, pos_blocks, ys, h1, gates_tk, g2.reshape(1, d), b2.reshape(1, d))


def _forward(cfg, x, ln_in_g, ln_in_b, w_in, conv_w, w_out, ln1_g, ln1_b, router_w, router_b,
             w_gate_up, b_gate_up, w_down, b_down, ln2_g, ln2_b):
    assert cfg.depth == 1
    b, s, d = x.shape
    x2d = x.reshape(b * s, d)
    slopes = jnp.asarray(_alibi_slopes(cfg.n_heads))
    h0 = _ln_in(x2d, ln_in_g, ln_in_b, cfg)
    proj = _matmul([h0], w_in[0], cfg, F32, "in_proj")
    attn = _attention(proj, slopes, cfg)
    y_conv = _gated_conv(proj, conv_w[0], cfg)
    mix = _matmul([attn, y_conv], w_out[0], cfg, F32, "out_proj")
    h1, top_idx, gates = _ln1_router(x2d, mix, ln_in_g, ln_in_b, ln1_g[0], ln1_b[0],
                                     router_w[0], router_b[0], cfg)
    row_tok, pos, *plan_arrays = _routing_plan(top_idx, cfg)
    xs = _gather_rows(h1, row_tok, plan_arrays[-1], cfg)
    act = _gate_up(xs, w_gate_up[0], b_gate_up[0], plan_arrays, cfg)
    ys = _down(act, w_down[0], b_down[0], plan_arrays, cfg)
    out = _combine(ys, pos, gates.T, h1, ln2_g[0], ln2_b[0], cfg)
    return out.reshape(b, s, d)


_CONFIG = Config(
    batch=4, seq=2048, d_model=4096, head_dim=128, n_heads=24, conv_width=1024,
    patterns=((128, 1), (512, 4), (2048, 16)), n_experts=32, top_k=4, d_expert=2048, depth=1,
    ln_rows=256, mm_tm=1024, mm_tn=512, conv_cw=256, moe_rows=256, gu_tf=512, dn_tn=2048,
    gather_rows=256, combine_rows=64)


def kernel(x, ln_in_g, ln_in_b, w_in, conv_w, w_out, ln1_g, ln1_b, router_w, router_b,
           w_gate_up, b_gate_up, w_down, b_down, ln2_g, ln2_b):
    return _forward(_CONFIG, x, ln_in_g, ln_in_b, w_in, conv_w, w_out, ln1_g, ln1_b, router_w, router_b,
                    w_gate_up, b_gate_up, w_down, b_down, ln2_g, ln2_b)
```

```python
import dataclasses
import functools
import math

import numpy as np
import jax
import jax.numpy as jnp
from jax import lax
from jax.experimental import pallas as pl
from jax.experimental.pallas import tpu as pltpu

F32 = jnp.float32
BF16 = jnp.bfloat16

LN_EPS = 1e-5
SWIGLU_LIMIT = 7.0
SWIGLU_ALPHA = 1.702
MASKED_SCORE = -1e30
V7X_VMEM_LIMIT_BYTES = 56 * 1024 * 1024
V7X_VMEM_GATHER_LIMIT_BYTES = 60 * 1024 * 1024


@dataclasses.dataclass(frozen=True)
class Config:
    batch: int
    seq: int
    d_model: int
    head_dim: int
    n_heads: int
    conv_width: int
    patterns: tuple
    n_experts: int
    top_k: int
    d_expert: int
    depth: int
    ln_rows: int
    mm_tm: int
    mm_tn: int
    conv_cw: int
    moe_rows: int
    gu_tf: int
    dn_tn: int
    gather_rows: int
    combine_rows: int

    @property
    def attn_width(self):
        return self.n_heads * self.head_dim

    @property
    def tokens(self):
        return self.batch * self.seq


def _params(sem, vmem=V7X_VMEM_LIMIT_BYTES):
    return pltpu.CompilerParams(dimension_semantics=sem, vmem_limit_bytes=vmem)


def _alibi_slopes(n):
    def pow2(m):
        start = 2.0 ** (-(2.0 ** -(math.log2(m) - 3)))
        return [start ** (i + 1) for i in range(m)]
    if math.log2(n).is_integer():
        s = pow2(n)
    else:
        c = 2 ** int(math.floor(math.log2(n)))
        s = pow2(c) + pow2(2 * c)[0::2][: n - c]
    return np.asarray(s, dtype=np.float32)


def _layer_norm_rows(x, g, b):
    mu = jnp.mean(x, axis=-1, keepdims=True)
    xc = x - mu
    var = jnp.mean(xc * xc, axis=-1, keepdims=True)
    return xc * lax.rsqrt(var + LN_EPS) * g + b


def _ln_in_kernel(x_ref, g_ref, b_ref, o_ref):
    o_ref[...] = _layer_norm_rows(x_ref[...], g_ref[...], b_ref[...]).astype(o_ref.dtype)


def _ln_in(x2d, g, b, cfg):
    t, d = x2d.shape
    tm = cfg.ln_rows
    row = pl.BlockSpec((tm, d), lambda i: (i, 0))
    vec = pl.BlockSpec((1, d), lambda i: (0, 0))
    return pl.pallas_call(
        _ln_in_kernel, grid=(t // tm,), in_specs=[row, vec, vec], out_specs=row,
        out_shape=jax.ShapeDtypeStruct((t, d), BF16),
        compiler_params=_params(("parallel",)), name="ln_in",
    )(x2d, g.reshape(1, d), b.reshape(1, d))


def _matmul_kernel(*refs):
    *xw_refs, o_ref = refs
    n_in = len(xw_refs) // 2
    acc = None
    for x_ref, w_ref in zip(xw_refs[:n_in], xw_refs[n_in:]):
        part = jnp.dot(x_ref[...], w_ref[...].astype(BF16), preferred_element_type=F32)
        acc = part if acc is None else acc + part
    o_ref[...] = acc.astype(o_ref.dtype)


def _matmul(xs, w, cfg, out_dtype, name):
    m = xs[0].shape[0]
    n = w.shape[1]
    tm, tn = cfg.mm_tm, cfg.mm_tn
    x_specs, w_specs, row0 = [], [], 0
    for x in xs:
        k = x.shape[1]
        assert row0 % k == 0
        x_specs.append(pl.BlockSpec((tm, k), lambda i, j: (i, 0)))
        w_specs.append(pl.BlockSpec((k, tn), lambda i, j, r=row0 // k: (r, j)))
        row0 += k
    assert row0 == w.shape[0]
    return pl.pallas_call(
        _matmul_kernel, grid=(m // tm, n // tn),
        in_specs=x_specs + w_specs,
        out_specs=pl.BlockSpec((tm, tn), lambda i, j: (i, j)),
        out_shape=jax.ShapeDtypeStruct((m, n), out_dtype),
        compiler_params=_params(("parallel", "parallel")), name=name,
    )(*xs, *([w] * len(xs)))


def _attn_kernel(slopes_ref, q_ref, k_ref, v_ref, o_ref, acc_ref, m_ref, l_ref, *, cfg, blk):
    seq, hd = cfg.seq, cfg.head_dim
    nblk = seq // blk
    slope = slopes_ref[pl.program_id(1)]
    scale = 1.0 / math.sqrt(hd)

    qi = lax.broadcasted_iota(jnp.int32, (blk, blk), 0)
    ki = lax.broadcasted_iota(jnp.int32, (blk, blk), 1)
    d_own = qi - ki
    own_steps = d_own.astype(F32)
    prev_steps = (d_own + blk).astype(F32)

    for g, (window, dil) in enumerate(cfg.patterns):
        sub = seq // dil
        nb = sub // blk
        neg = -slope * float(dil)

        def to_blocks(ref, f, dil=dil, sub=sub, nb=nb):
            if dil == 1:
                return f(ref[...]).reshape(nblk, blk, hd)
            return jnp.concatenate(
                [f(ref[pl.ds(r, sub, stride=dil), :]).reshape(nb, blk, hd) for r in range(dil)], axis=0)

        def from_blocks(dst, x, g=g, dil=dil, sub=sub, nb=nb):
            if dil == 1:
                dst[g] = x.reshape(seq, hd)
            else:
                for r in range(dil):
                    dst[g, pl.ds(r, sub, stride=dil), :] = x[r * nb:(r + 1) * nb].reshape(sub, hd)

        q = to_blocks(q_ref, lambda x: (x * scale).astype(BF16))
        keys = to_blocks(k_ref, lambda x: x.astype(BF16))
        vals = to_blocks(v_ref, lambda x: x.astype(BF16))
        bias = jnp.where(d_own >= 0, own_steps * neg, MASKED_SCORE)[None]
        if nb > 1:
            def with_prev(x, dil=dil, nb=nb):
                x4 = x.reshape(dil, nb, blk, hd)
                prev = jnp.concatenate([jnp.zeros((dil, 1, blk, hd), x.dtype), x4[:, :-1]], axis=1)
                return jnp.concatenate([prev.reshape(nblk, blk, hd), x], axis=1)
            keys, vals = with_prev(keys), with_prev(vals)
            n_of_block = lax.broadcasted_iota(jnp.int32, (nblk, blk, blk), 0) & (nb - 1)
            prev_ok = jnp.logical_and(d_own[None] <= 0, n_of_block > 0)
            bias_prev = jnp.where(prev_ok, (prev_steps * neg)[None], MASKED_SCORE)
            bias = jnp.concatenate([bias_prev, jnp.broadcast_to(bias, (nblk, blk, blk))], axis=2)
        s = jnp.einsum("bqd,bkd->bqk", q, keys, preferred_element_type=F32) + bias
        m = jnp.max(s, axis=-1, keepdims=True)
        p = jnp.exp(s - m).astype(BF16)
        vals1 = jnp.concatenate([vals, jnp.ones(vals.shape, BF16)], axis=-1)
        pv = jnp.einsum("bqk,bkd->bqd", p, vals1, preferred_element_type=F32)
        from_blocks(acc_ref, pv[:, :, :hd])
        from_blocks(l_ref, pv[:, :, hd:])
        from_blocks(m_ref, jnp.broadcast_to(m, (nblk, blk, hd)))

    n_win = len(cfg.patterns)
    gs = max(c for c in range(1, 5) if nblk % c == 0)

    def merge(i, carry):
        for j in range(gs):
            r = pl.ds(pl.multiple_of((i * gs + j) * blk, blk), blk)
            ms = [m_ref[g, r, :] for g in range(n_win)]
            m_all = functools.reduce(jnp.maximum, ms)
            ws = [jnp.exp(m - m_all) for m in ms]
            num = functools.reduce(lambda a, b: a + b, [w * acc_ref[g, r, :] for g, w in enumerate(ws)])
            den = functools.reduce(lambda a, b: a + b, [w * l_ref[g, r, :] for g, w in enumerate(ws)])
            o_ref[r, :] = (num / den).astype(o_ref.dtype)
        return carry
    lax.fori_loop(0, nblk // gs, merge, 0)


def _attention(proj, slopes, cfg):
    seq, hd, nh = cfg.seq, cfg.head_dim, cfg.n_heads
    blks = {w // d for w, d in cfg.patterns}
    assert len(blks) == 1, "all windows must share one block length"
    blk = blks.pop()
    for w, d in cfg.patterns:
        nb = seq // (d * blk)
        assert d & (d - 1) == 0 and nb & (nb - 1) == 0 and nb * d * blk == seq
    grid_spec = pltpu.PrefetchScalarGridSpec(
        num_scalar_prefetch=1, grid=(cfg.batch, nh),
        in_specs=[pl.BlockSpec((seq, hd), lambda b, h, s: (b, h)),
                  pl.BlockSpec((seq, hd), lambda b, h, s: (b, nh + h)),
                  pl.BlockSpec((seq, hd), lambda b, h, s: (b, 2 * nh + h))],
        out_specs=pl.BlockSpec((seq, hd), lambda b, h, s: (b, h)),
        scratch_shapes=[pltpu.VMEM((len(cfg.patterns), seq, hd), F32)] * 3)
    return pl.pallas_call(
        functools.partial(_attn_kernel, cfg=cfg, blk=blk), grid_spec=grid_spec,
        out_shape=jax.ShapeDtypeStruct((cfg.tokens, cfg.attn_width), BF16),
        compiler_params=_params(("parallel", "parallel")), name="dilated_attention",
    )(slopes, proj, proj, proj)


def _conv_kernel(gb_ref, gc_ref, val_ref, w_ref, o_ref):
    u = gc_ref[...] * val_ref[...]
    row = lax.broadcasted_iota(jnp.int32, u.shape, 0)
    u1 = jnp.where(row >= 1, pltpu.roll(u, 1, 0), 0.0)
    u2 = jnp.where(row >= 2, pltpu.roll(u, 2, 0), 0.0)
    w = w_ref[...]
    z = w[0:1, :] * u2 + w[1:2, :] * u1 + w[2:3, :] * u
    o_ref[...] = (gb_ref[...] * z).astype(o_ref.dtype)


def _gated_conv(proj, conv_w, cfg):
    seq, cw, c = cfg.seq, cfg.conv_cw, cfg.conv_width
    base = 3 * cfg.attn_width // cw
    per = c // cw
    spec = lambda off: pl.BlockSpec((seq, cw), lambda b, j, off=off: (b, base + off * per + j))
    return pl.pallas_call(
        _conv_kernel, grid=(cfg.batch, per),
        in_specs=[spec(0), spec(1), spec(2), pl.BlockSpec((conv_w.shape[0], cw), lambda b, j: (0, j))],
        out_specs=pl.BlockSpec((seq, cw), lambda b, j: (b, j)),
        out_shape=jax.ShapeDtypeStruct((cfg.tokens, c), BF16),
        compiler_params=_params(("parallel", "parallel")), name="gated_conv",
    )(proj, proj, proj, conv_w)


def _ln1_router_kernel(x_ref, mix_ref, gi_ref, bi_ref, g1_ref, b1_ref, rw_ref, rb_ref,
                       h_ref, idx_ref, gate_ref, *, alpha, top_k):
    h0 = _layer_norm_rows(x_ref[...], gi_ref[...], bi_ref[...])
    h1 = _layer_norm_rows(alpha * h0 + mix_ref[...], g1_ref[...], b1_ref[...])
    h_ref[...] = h1
    def split(v):
        hi = v.astype(BF16)
        return hi, (v - hi.astype(F32)).astype(BF16)

    def dot_t(a, b):
        return lax.dot_general(a, b, (((1,), (1,)), ((), ())), preferred_element_type=F32)
    w_hi, w_lo = split(rw_ref[...])
    h_hi, h_lo = split(h1)
    logits = dot_t(w_hi, h_hi) + (dot_t(w_hi, h_lo) + dot_t(w_lo, h_hi)) + rb_ref[...]
    n_e = logits.shape[0]
    eidx = lax.broadcasted_iota(jnp.int32, logits.shape, 0)
    vals, idxs = [], []
    for _ in range(top_k):
        m = jnp.max(logits, axis=0, keepdims=True)
        i = jnp.min(jnp.where(logits == m, eidx, n_e), axis=0, keepdims=True)
        vals.append(m)
        idxs.append(i)
        logits = jnp.where(eidx == i, -jnp.inf, logits)
    ex = [jnp.exp(v - vals[0]) for v in vals]
    den = functools.reduce(lambda a, b: a + b, ex)
    idx_ref[...] = jnp.concatenate(idxs, axis=0)
    gate_ref[...] = jnp.concatenate([e / den for e in ex], axis=0)


def _ln1_router(x2d, mix, gi, bi, g1, b1, router_w, router_b, cfg):
    t, d = x2d.shape
    e, k = cfg.n_experts, cfg.top_k
    tm = cfg.ln_rows
    alpha = (2 * cfg.depth) ** 0.25
    row = pl.BlockSpec((tm, d), lambda i: (i, 0))
    vec = pl.BlockSpec((1, d), lambda i: (0, 0))
    sel = pl.BlockSpec((k, tm), lambda i: (0, i))
    return pl.pallas_call(
        functools.partial(_ln1_router_kernel, alpha=alpha, top_k=k), grid=(t // tm,),
        in_specs=[row, row, vec, vec, vec, vec,
                  pl.BlockSpec((e, d), lambda i: (0, 0)), pl.BlockSpec((e, 1), lambda i: (0, 0))],
        out_specs=[row, sel, sel],
        out_shape=[jax.ShapeDtypeStruct((t, d), F32), jax.ShapeDtypeStruct((k, t), jnp.int32),
                   jax.ShapeDtypeStruct((k, t), F32)],
        compiler_params=_params(("parallel",)), name="ln1_router",
    )(x2d, mix, gi.reshape(1, d), bi.reshape(1, d), g1.reshape(1, d), b1.reshape(1, d),
      router_w.T, router_b.reshape(e, 1))


def _routing_plan(top_idx, cfg):
    k, t = top_idx.shape
    e, rb = cfg.n_experts, cfg.moe_rows
    a = k * t
    nb = a // rb + e
    p = nb * rb
    i32 = jnp.int32
    flat_e = top_idx.reshape(a)
    slot = jnp.arange(a, dtype=i32)
    experts = jnp.arange(e, dtype=i32)
    sorted_e, order = lax.sort((flat_e, slot), num_keys=1)
    starts = jnp.sum((flat_e[:, None] < experts[None, :]).astype(i32), axis=0)
    counts = jnp.sum((flat_e[:, None] == experts[None, :]).astype(i32), axis=0)
    padded = (counts + rb - 1) // rb * rb
    pends = jnp.cumsum(padded)
    pstarts = pends - padded
    pos_sorted = pstarts[sorted_e] + slot - starts[sorted_e]
    _, pos = lax.sort((order, pos_sorted), num_keys=1)

    rows = jnp.arange(p, dtype=i32)
    row_e = jnp.minimum(jnp.sum((rows[:, None] >= pends[None, :]).astype(i32), axis=1), e - 1)
    local = rows - pstarts[row_e]
    src = jnp.clip(starts[row_e] + local, 0, a - 1)
    row_tok = jnp.where(local < counts[row_e], order[src] % t, 0).astype(i32)

    n_blocks = pends[-1] // rb
    bi = jnp.arange(nb, dtype=i32)
    raw_e = row_e[::rb]
    block_e = jnp.where(bi < n_blocks, raw_e, raw_e[n_blocks - 1])
    first = jnp.concatenate([jnp.ones((1,), i32), (block_e[1:] != block_e[:-1]).astype(i32)])
    tile_ord = jnp.cumsum(first) - 1
    tile_e = jnp.max(jnp.where(tile_ord[None, :] == bi[:, None], block_e[None, :], -1), axis=1)
    next_e = tile_e[jnp.minimum(tile_ord + 1, nb - 1)]
    meta = jnp.stack([n_blocks, tile_ord[-1] + 1]).astype(i32)
    return (row_tok, pos.reshape(k, t).astype(i32), block_e.astype(i32), first, tile_ord.astype(i32),
            next_e.astype(i32), meta)


ROW_DMA_UNROLL = 8


def _row_copy(src_hbm, dst_vmem, sem, src_row, dst_row):
    return pltpu.make_async_copy(src_hbm.at[pl.ds(src_row, 1), :], dst_vmem.at[pl.ds(dst_row, 1), :], sem)


def _start_rows(src_hbm, dst_vmem, sem, n, src_row_of):
    def body(c, carry):
        for u in range(ROW_DMA_UNROLL):
            r = c * ROW_DMA_UNROLL + u
            _row_copy(src_hbm, dst_vmem, sem, src_row_of(r), r).start(priority=u % 2)
        return carry
    lax.fori_loop(0, n // ROW_DMA_UNROLL, body, 0)


def _wait_rows(src_hbm, dst_vmem, sem, n):
    def body(c, carry):
        for u in range(ROW_DMA_UNROLL):
            _row_copy(src_hbm, dst_vmem, sem, 0, c * ROW_DMA_UNROLL + u).wait()
        return carry
    lax.fori_loop(0, n // ROW_DMA_UNROLL, body, 0)


WEIGHT_CAST_CHUNKS = 4
WEIGHT_DMA_PRIORITY = 1


class _TilePlan:
    def __init__(self, be_ref, first_ref, ord_ref, next_ref, meta_ref):
        j, i = pl.program_id(0), pl.program_id(1)
        self.j, self.i = j, i
        self.expert = be_ref[i]
        self.first = first_ref[i] == 1
        self.valid = i < meta_ref[0]
        tile = j * meta_ref[1] + ord_ref[i]
        self.slot = tile & 1
        self.is_tile0 = tile == 0
        nxt = next_ref[i]
        more_j = j + 1 < pl.num_programs(0)
        self.has_next = jnp.logical_or(nxt >= 0, more_j)
        self.next_e = jnp.where(nxt >= 0, nxt, be_ref[0])
        self.next_j = jnp.where(nxt >= 0, j, j + 1)


def _first_step_dot(x, stage, w_s):
    k = x.shape[1]
    ck = k // WEIGHT_CAST_CHUNKS
    acc = None
    for c in range(WEIGHT_CAST_CHUNKS):
        wb = stage[pl.ds(c * ck, ck), :].astype(BF16)
        w_s[pl.ds(c * ck, ck), :] = wb
        part = jnp.dot(x[:, c * ck:(c + 1) * ck], wb, preferred_element_type=F32)
        acc = part if acc is None else acc + part
    return acc


def _swiglu(g, u):
    g = jnp.minimum(g, SWIGLU_LIMIT)
    u = jnp.clip(u, -SWIGLU_LIMIT, SWIGLU_LIMIT)
    return (u + 1.0) * (g * jax.nn.sigmoid(SWIGLU_ALPHA * g))


def _gate_up_kernel(be_ref, first_ref, ord_ref, next_ref, meta_ref, *refs, tf, f, j0, gather):
    if gather:
        (tok_ref, tok_next_ref, h_hbm, w_hbm, bg_ref, bu_ref, x_ref, o_ref,
         rows, stage_g, stage_u, wg_s, wu_s, row_sem, sem) = refs
    else:
        x_ref, w_hbm, bg_ref, bu_ref, o_ref, stage_g, stage_u, wg_s, wu_s, sem = refs
    plan = _TilePlan(be_ref, first_ref, ord_ref, next_ref, meta_ref)

    if gather:
        n = rows.shape[1]
        slot = plan.i % 2

        @pl.when(plan.i == 0)
        def _():
            _start_rows(h_hbm, rows.at[0], row_sem.at[0], n, lambda r: tok_ref[0, r])

        @pl.when(plan.i + 1 < meta_ref[0])
        def _():
            _start_rows(h_hbm, rows.at[1 - slot], row_sem.at[1 - slot], n, lambda r: tok_next_ref[0, r])

        @pl.when(plan.valid)
        def _():
            _wait_rows(h_hbm, rows.at[slot], row_sem.at[slot], n)
            x_ref[...] = rows[slot].astype(x_ref.dtype)

        @pl.when(jnp.logical_not(plan.valid))
        def _():
            x_ref[...] = jnp.zeros(x_ref.shape, x_ref.dtype)

    def tile_copies(e, j, slot):
        col = pl.multiple_of((j + j0) * tf, tf)
        return (pltpu.make_async_copy(w_hbm.at[e, :, pl.ds(col, tf)], stage_g.at[slot], sem.at[slot]),
                pltpu.make_async_copy(w_hbm.at[e, :, pl.ds(f + col, tf)], stage_u.at[slot], sem.at[slot]))

    @pl.when(plan.first)
    def _():
        @pl.when(plan.is_tile0)
        def _():
            for c in tile_copies(plan.expert, plan.j, plan.slot):
                c.start(priority=WEIGHT_DMA_PRIORITY)
        for c in tile_copies(plan.expert, plan.j, plan.slot):
            c.wait()

        @pl.when(plan.has_next)
        def _():
            for c in tile_copies(plan.next_e, plan.next_j, 1 - plan.slot):
                c.start(priority=WEIGHT_DMA_PRIORITY)
        x = x_ref[...]
        g = _first_step_dot(x, stage_g.at[plan.slot], wg_s) + bg_ref[...]
        u = _first_step_dot(x, stage_u.at[plan.slot], wu_s) + bu_ref[...]
        o_ref[...] = _swiglu(g, u).astype(o_ref.dtype)

    @pl.when(jnp.logical_and(jnp.logical_not(plan.first), plan.valid))
    def _():
        x = x_ref[...]
        g = jnp.dot(x, wg_s[...], preferred_element_type=F32) + bg_ref[...]
        u = jnp.dot(x, wu_s[...], preferred_element_type=F32) + bu_ref[...]
        o_ref[...] = _swiglu(g, u).astype(o_ref.dtype)

    @pl.when(jnp.logical_not(plan.valid))
    def _():
        o_ref[...] = jnp.zeros(o_ref.shape, o_ref.dtype)


def _gate_up(x_src, w_gu, b_gu, plan_arrays, cfg, j0, nj, row_tok=None):
    gather = row_tok is not None
    e, f, rb, tf = cfg.n_experts, cfg.d_expert, cfg.moe_rows, cfg.gu_tf
    d = x_src.shape[1]
    nf = f // tf
    p = row_tok.shape[0] if gather else x_src.shape[0]
    nb = p // rb
    assert d % (WEIGHT_CAST_CHUNKS * 128) == 0 and rb % ROW_DMA_UNROLL == 0
    bias = lambda off: pl.BlockSpec((None, 1, tf), lambda j, i, be, *_: (be[i], 0, off + j0 + j))
    act_spec = pl.BlockSpec((rb, tf), lambda j, i, be, *_: (i, j))
    row_spec = pl.BlockSpec((rb, d), lambda j, i, be, *_: (i, 0))
    weights = [pltpu.VMEM((2, d, tf), F32), pltpu.VMEM((2, d, tf), F32),
               pltpu.VMEM((d, tf), BF16), pltpu.VMEM((d, tf), BF16)]
    act_shape = jax.ShapeDtypeStruct((p, nj * tf), BF16)
    b3 = b_gu.reshape(e, 1, 2 * f)
    if gather:
        assert nj == 1
        tok = row_tok.reshape(nb, 1, rb)
        idx = lambda g: pl.BlockSpec((None, 1, rb), lambda j, i, *_: (g(i), 0, 0), memory_space=pltpu.SMEM)
        in_specs = [idx(lambda i: i), idx(lambda i: jnp.minimum(i + 1, nb - 1)),
                    pl.BlockSpec(memory_space=pl.ANY), pl.BlockSpec(memory_space=pl.ANY), bias(0), bias(nf)]
        out_specs = [row_spec, act_spec]
        out_shape = [jax.ShapeDtypeStruct((p, d), BF16), act_shape]
        scratch = [pltpu.VMEM((2, rb, d), F32)] + weights + [pltpu.SemaphoreType.DMA((2,))] * 2
        operands = (tok, tok, x_src, w_gu, b3, b3)
        vmem = V7X_VMEM_GATHER_LIMIT_BYTES
    else:
        in_specs = [row_spec, pl.BlockSpec(memory_space=pl.ANY), bias(0), bias(nf)]
        out_specs, out_shape = act_spec, act_shape
        scratch = weights + [pltpu.SemaphoreType.DMA((2,))]
        operands = (x_src, w_gu, b3, b3)
        vmem = V7X_VMEM_LIMIT_BYTES
    grid_spec = pltpu.PrefetchScalarGridSpec(
        num_scalar_prefetch=5, grid=(nj, nb), in_specs=in_specs, out_specs=out_specs, scratch_shapes=scratch)
    return pl.pallas_call(
        functools.partial(_gate_up_kernel, tf=tf, f=f, j0=j0, gather=gather), grid_spec=grid_spec,
        out_shape=out_shape, compiler_params=_params(("arbitrary", "arbitrary"), vmem),
        name="moe_gather_gate_up" if gather else "moe_gate_up",
    )(*plan_arrays, *operands)


def _down_kernel(be_ref, first_ref, ord_ref, next_ref, meta_ref, *refs, tn):
    *a_refs, w_hbm, b_ref, o_ref, stage, w_s, sem = refs
    plan = _TilePlan(be_ref, first_ref, ord_ref, next_ref, meta_ref)

    def act():
        return jnp.concatenate([r[...] for r in a_refs], axis=1) if len(a_refs) > 1 else a_refs[0][...]

    def tile_copy(e, j, slot):
        col = pl.multiple_of(j * tn, tn)
        return pltpu.make_async_copy(w_hbm.at[e, :, pl.ds(col, tn)], stage.at[slot], sem.at[slot])

    @pl.when(plan.first)
    def _():
        @pl.when(plan.is_tile0)
        def _():
            tile_copy(plan.expert, plan.j, plan.slot).start(priority=WEIGHT_DMA_PRIORITY)
        tile_copy(plan.expert, plan.j, plan.slot).wait()

        @pl.when(plan.has_next)
        def _():
            tile_copy(plan.next_e, plan.next_j, 1 - plan.slot).start(priority=WEIGHT_DMA_PRIORITY)
        o_ref[...] = _first_step_dot(act(), stage.at[plan.slot], w_s) + b_ref[...]

    @pl.when(jnp.logical_and(jnp.logical_not(plan.first), plan.valid))
    def _():
        o_ref[...] = jnp.dot(act(), w_s[...], preferred_element_type=F32) + b_ref[...]

    @pl.when(jnp.logical_not(plan.valid))
    def _():
        o_ref[...] = jnp.zeros(o_ref.shape, o_ref.dtype)


def _down(acts, w_dn, b_dn, plan_arrays, cfg):
    p = acts[0].shape[0]
    f = sum(a.shape[1] for a in acts)
    e, d, rb, tn = cfg.n_experts, cfg.d_model, cfg.moe_rows, cfg.dn_tn
    assert f == cfg.d_expert and f % (WEIGHT_CAST_CHUNKS * 128) == 0
    grid_spec = pltpu.PrefetchScalarGridSpec(
        num_scalar_prefetch=5, grid=(d // tn, p // rb),
        in_specs=[pl.BlockSpec((rb, a.shape[1]), lambda j, i, be, *_: (i, 0)) for a in acts] + [
                  pl.BlockSpec(memory_space=pl.ANY),
                  pl.BlockSpec((None, 1, tn), lambda j, i, be, *_: (be[i], 0, j))],
        out_specs=pl.BlockSpec((rb, tn), lambda j, i, be, *_: (i, j)),
        scratch_shapes=[pltpu.VMEM((2, f, tn), F32), pltpu.VMEM((f, tn), BF16),
                        pltpu.SemaphoreType.DMA((2,))])
    return pl.pallas_call(
        functools.partial(_down_kernel, tn=tn), grid_spec=grid_spec,
        out_shape=jax.ShapeDtypeStruct((p, d), F32),
        compiler_params=_params(("arbitrary", "arbitrary")), name="moe_down",
    )(*plan_arrays, *acts, w_dn, b_dn.reshape(e, 1, d))


def _combine_kernel(pos_ref, pos_next_ref, ys_hbm, h_ref, gate_ref, g2_ref, b2_ref, o_ref, buf, sem, *, alpha):
    i = pl.program_id(0)
    top_k, n = buf.shape[1], buf.shape[2]
    slot = i % 2

    def start(idx_ref, s):
        for k in range(top_k):
            _start_rows(ys_hbm, buf.at[s, k], sem.at[s], n, lambda t, k=k: idx_ref[k, t])

    @pl.when(i == 0)
    def _():
        start(pos_ref, 0)

    @pl.when(i + 1 < pl.num_programs(0))
    def _():
        start(pos_next_ref, 1 - slot)

    for k in range(top_k):
        _wait_rows(ys_hbm, buf.at[slot, k], sem.at[slot], n)
    gates = gate_ref[...]
    y = gates[:, 0:1] * buf[slot, 0]
    for k in range(1, top_k):
        y = y + gates[:, k:k + 1] * buf[slot, k]
    o_ref[...] = _layer_norm_rows(alpha * h_ref[...] + y, g2_ref[...], b2_ref[...])


def _combine(ys, pos, gates_tk, h1, g2, b2, cfg):
    t, d = h1.shape
    k, n = cfg.top_k, cfg.combine_rows
    steps = t // n
    assert n % ROW_DMA_UNROLL == 0
    alpha = (2 * cfg.depth) ** 0.25
    pos_blocks = pos.reshape(k, steps, n).transpose(1, 0, 2)
    idx = lambda f: pl.BlockSpec((None, k, n), lambda i: (f(i), 0, 0), memory_space=pltpu.SMEM)
    row = pl.BlockSpec((n, d), lambda i: (i, 0))
    vec = pl.BlockSpec((1, d), lambda i: (0, 0))
    return pl.pallas_call(
        functools.partial(_combine_kernel, alpha=alpha), grid=(steps,),
        in_specs=[idx(lambda i: i), idx(lambda i: jnp.minimum(i + 1, steps - 1)),
                  pl.BlockSpec(memory_space=pl.ANY), row,
                  pl.BlockSpec((n, k), lambda i: (i, 0)), vec, vec],
        out_specs=row,
        out_shape=jax.ShapeDtypeStruct((t, d), F32),
        scratch_shapes=[pltpu.VMEM((2, k, n, d), F32), pltpu.SemaphoreType.DMA((2,))],
        compiler_params=_params(("arbitrary",)), name="moe_combine_ln2",
    )(pos_blocks, pos_blocks, ys, h1, gates_tk, g2.reshape(1, d), b2.reshape(1, d))


def _forward(cfg, x, ln_in_g, ln_in_b, w_in, conv_w, w_out, ln1_g, ln1_b, router_w, router_b,
             w_gate_up, b_gate_up, w_down, b_down, ln2_g, ln2_b):
    assert cfg.depth == 1
    b, s, d = x.shape
    x2d = x.reshape(b * s, d)
    slopes = jnp.asarray(_alibi_slopes(cfg.n_heads))
    h0 = _ln_in(x2d, ln_in_g, ln_in_b, cfg)
    proj = _matmul([h0], w_in[0], cfg, F32, "in_proj")
    attn = _attention(proj, slopes, cfg)
    y_conv = _gated_conv(proj, conv_w[0], cfg)
    mix = _matmul([attn, y_conv], w_out[0], cfg, F32, "out_proj")
    h1, top_idx, gates = _ln1_router(x2d, mix, ln_in_g, ln_in_b, ln1_g[0], ln1_b[0],
                                     router_w[0], router_b[0], cfg)
    row_tok, pos, *plan_arrays = _routing_plan(top_idx, cfg)
    n_tiles = cfg.d_expert // cfg.gu_tf
    xs, act0 = _gate_up(h1, w_gate_up[0], b_gate_up[0], plan_arrays, cfg, 0, 1, row_tok=row_tok)
    acts = [act0]
    if n_tiles > 1:
        acts.append(_gate_up(xs, w_gate_up[0], b_gate_up[0], plan_arrays, cfg, 1, n_tiles - 1))
    ys = _down(acts, w_down[0], b_down[0], plan_arrays, cfg)
    out = _combine(ys, pos, gates.T, h1, ln2_g[0], ln2_b[0], cfg)
    return out.reshape(b, s, d)


_CONFIG = Config(
    batch=4, seq=2048, d_model=4096, head_dim=128, n_heads=24, conv_width=1024,
    patterns=((128, 1), (512, 4), (2048, 16)), n_experts=32, top_k=4, d_expert=2048, depth=1,
    ln_rows=256, mm_tm=1024, mm_tn=512, conv_cw=256, moe_rows=256, gu_tf=512, dn_tn=2048,
    gather_rows=256, combine_rows=64)


def kernel(x, ln_in_g, ln_in_b, w_in, conv_w, w_out, ln1_g, ln1_b, router_w, router_b,
           w_gate_up, b_gate_up, w_down, b_down, ln2_g, ln2_b):
    return _forward(_CONFIG, x, ln_in_g, ln_in_b, w_in, conv_w, w_out, ln1_g, ln1_b, router_w, router_b,
                    w_gate_up, b_gate_up, w_down, b_down, ln2_g, ln2_b)
```

```python
import dataclasses
import functools
import math

import numpy as np
import jax
import jax.numpy as jnp
from jax import lax
from jax.experimental import pallas as pl
from jax.experimental.pallas import tpu as pltpu

F32 = jnp.float32
BF16 = jnp.bfloat16

LN_EPS = 1e-5
SWIGLU_LIMIT = 7.0
SWIGLU_ALPHA = 1.702
MASKED_SCORE = -1e30
V7X_VMEM_LIMIT_BYTES = 56 * 1024 * 1024


@dataclasses.dataclass(frozen=True)
class Config:
    batch: int
    seq: int
    d_model: int
    head_dim: int
    n_heads: int
    conv_width: int
    patterns: tuple
    n_experts: int
    top_k: int
    d_expert: int
    depth: int
    ln_rows: int
    mm_tm: int
    mm_tn: int
    conv_cw: int
    moe_rows: int
    gu_tf: int
    dn_tn: int
    gather_rows: int
    combine_rows: int

    @property
    def attn_width(self):
        return self.n_heads * self.head_dim

    @property
    def tokens(self):
        return self.batch * self.seq


def _params(sem, vmem=V7X_VMEM_LIMIT_BYTES):
    return pltpu.CompilerParams(dimension_semantics=sem, vmem_limit_bytes=vmem)


def _alibi_slopes(n):
    def pow2(m):
        start = 2.0 ** (-(2.0 ** -(math.log2(m) - 3)))
        return [start ** (i + 1) for i in range(m)]
    if math.log2(n).is_integer():
        s = pow2(n)
    else:
        c = 2 ** int(math.floor(math.log2(n)))
        s = pow2(c) + pow2(2 * c)[0::2][: n - c]
    return np.asarray(s, dtype=np.float32)


def _pack_halves(x):
    n = x.shape[1] // 2
    return pltpu.pack_elementwise([x[:, :n], x[:, n:]], packed_dtype=BF16)


def _unpack_half(words, half):
    return pltpu.unpack_elementwise(words, index=half, packed_dtype=BF16, unpacked_dtype=F32)


def _layer_norm_rows(x, g, b):
    mu = jnp.mean(x, axis=-1, keepdims=True)
    xc = x - mu
    var = jnp.mean(xc * xc, axis=-1, keepdims=True)
    return xc * lax.rsqrt(var + LN_EPS) * g + b


def _ln_in_kernel(x_ref, g_ref, b_ref, o_ref):
    o_ref[...] = _layer_norm_rows(x_ref[...], g_ref[...], b_ref[...]).astype(o_ref.dtype)


def _ln_in(x2d, g, b, cfg):
    t, d = x2d.shape
    tm = cfg.ln_rows
    row = pl.BlockSpec((tm, d), lambda i: (i, 0))
    vec = pl.BlockSpec((1, d), lambda i: (0, 0))
    return pl.pallas_call(
        _ln_in_kernel, grid=(t // tm,), in_specs=[row, vec, vec], out_specs=row,
        out_shape=jax.ShapeDtypeStruct((t, d), BF16),
        compiler_params=_params(("parallel",)), name="ln_in",
    )(x2d, g.reshape(1, d), b.reshape(1, d))


def _matmul_kernel(*refs):
    *xw_refs, o_ref = refs
    n_in = len(xw_refs) // 2
    acc = None
    for x_ref, w_ref in zip(xw_refs[:n_in], xw_refs[n_in:]):
        part = jnp.dot(x_ref[...], w_ref[...].astype(BF16), preferred_element_type=F32)
        acc = part if acc is None else acc + part
    o_ref[...] = acc.astype(o_ref.dtype)


def _matmul(xs, w, cfg, out_dtype, name):
    m = xs[0].shape[0]
    n = w.shape[1]
    tm, tn = cfg.mm_tm, cfg.mm_tn
    x_specs, w_specs, row0 = [], [], 0
    for x in xs:
        k = x.shape[1]
        assert row0 % k == 0
        x_specs.append(pl.BlockSpec((tm, k), lambda i, j: (i, 0)))
        w_specs.append(pl.BlockSpec((k, tn), lambda i, j, r=row0 // k: (r, j)))
        row0 += k
    assert row0 == w.shape[0]
    return pl.pallas_call(
        _matmul_kernel, grid=(m // tm, n // tn),
        in_specs=x_specs + w_specs,
        out_specs=pl.BlockSpec((tm, tn), lambda i, j: (i, j)),
        out_shape=jax.ShapeDtypeStruct((m, n), out_dtype),
        compiler_params=_params(("parallel", "parallel")), name=name,
    )(*xs, *([w] * len(xs)))


def _attn_kernel(slopes_ref, q_ref, k_ref, v_ref, o_ref, acc_ref, m_ref, l_ref, *, cfg, blk):
    seq, hd = cfg.seq, cfg.head_dim
    nblk = seq // blk
    slope = slopes_ref[pl.program_id(1)]
    scale = 1.0 / math.sqrt(hd)

    qi = lax.broadcasted_iota(jnp.int32, (blk, blk), 0)
    ki = lax.broadcasted_iota(jnp.int32, (blk, blk), 1)
    d_own = qi - ki
    own_steps = d_own.astype(F32)
    prev_steps = (d_own + blk).astype(F32)

    for g, (window, dil) in enumerate(cfg.patterns):
        sub = seq // dil
        nb = sub // blk
        neg = -slope * float(dil)

        def to_blocks(ref, f, dil=dil, sub=sub, nb=nb):
            if dil == 1:
                return f(ref[...]).reshape(nblk, blk, hd)
            return jnp.concatenate(
                [f(ref[pl.ds(r, sub, stride=dil), :]).reshape(nb, blk, hd) for r in range(dil)], axis=0)

        def from_blocks(dst, x, g=g, dil=dil, sub=sub, nb=nb):
            if dil == 1:
                dst[g] = x.reshape(seq, hd)
            else:
                for r in range(dil):
                    dst[g, pl.ds(r, sub, stride=dil), :] = x[r * nb:(r + 1) * nb].reshape(sub, hd)

        q = to_blocks(q_ref, lambda x: (x * scale).astype(BF16))
        keys = to_blocks(k_ref, lambda x: x.astype(BF16))
        vals = to_blocks(v_ref, lambda x: x.astype(BF16))
        bias = jnp.where(d_own >= 0, own_steps * neg, MASKED_SCORE)[None]
        if nb > 1:
            def with_prev(x, dil=dil, nb=nb):
                x4 = x.reshape(dil, nb, blk, hd)
                prev = jnp.concatenate([jnp.zeros((dil, 1, blk, hd), x.dtype), x4[:, :-1]], axis=1)
                return jnp.concatenate([prev.reshape(nblk, blk, hd), x], axis=1)
            keys, vals = with_prev(keys), with_prev(vals)
            n_of_block = lax.broadcasted_iota(jnp.int32, (nblk, blk, blk), 0) & (nb - 1)
            prev_ok = jnp.logical_and(d_own[None] <= 0, n_of_block > 0)
            bias_prev = jnp.where(prev_ok, (prev_steps * neg)[None], MASKED_SCORE)
            bias = jnp.concatenate([bias_prev, jnp.broadcast_to(bias, (nblk, blk, blk))], axis=2)
        s = jnp.einsum("bqd,bkd->bqk", q, keys, preferred_element_type=F32) + bias
        m = jnp.max(s, axis=-1, keepdims=True)
        p = jnp.exp(s - m).astype(BF16)
        vals1 = jnp.concatenate([vals, jnp.ones(vals.shape, BF16)], axis=-1)
        pv = jnp.einsum("bqk,bkd->bqd", p, vals1, preferred_element_type=F32)
        from_blocks(acc_ref, pv[:, :, :hd])
        from_blocks(l_ref, pv[:, :, hd:])
        from_blocks(m_ref, jnp.broadcast_to(m, (nblk, blk, hd)))

    n_win = len(cfg.patterns)
    gs = max(c for c in range(1, 5) if nblk % c == 0)

    def merge(i, carry):
        for j in range(gs):
            r = pl.ds(pl.multiple_of((i * gs + j) * blk, blk), blk)
            ms = [m_ref[g, r, :] for g in range(n_win)]
            m_all = functools.reduce(jnp.maximum, ms)
            ws = [jnp.exp(m - m_all) for m in ms]
            num = functools.reduce(lambda a, b: a + b, [w * acc_ref[g, r, :] for g, w in enumerate(ws)])
            den = functools.reduce(lambda a, b: a + b, [w * l_ref[g, r, :] for g, w in enumerate(ws)])
            o_ref[r, :] = (num / den).astype(o_ref.dtype)
        return carry
    lax.fori_loop(0, nblk // gs, merge, 0)


def _attention(proj, slopes, cfg):
    seq, hd, nh = cfg.seq, cfg.head_dim, cfg.n_heads
    blks = {w // d for w, d in cfg.patterns}
    assert len(blks) == 1, "all windows must share one block length"
    blk = blks.pop()
    for w, d in cfg.patterns:
        nb = seq // (d * blk)
        assert d & (d - 1) == 0 and nb & (nb - 1) == 0 and nb * d * blk == seq
    grid_spec = pltpu.PrefetchScalarGridSpec(
        num_scalar_prefetch=1, grid=(cfg.batch, nh),
        in_specs=[pl.BlockSpec((seq, hd), lambda b, h, s: (b, h)),
                  pl.BlockSpec((seq, hd), lambda b, h, s: (b, nh + h)),
                  pl.BlockSpec((seq, hd), lambda b, h, s: (b, 2 * nh + h))],
        out_specs=pl.BlockSpec((seq, hd), lambda b, h, s: (b, h)),
        scratch_shapes=[pltpu.VMEM((len(cfg.patterns), seq, hd), F32)] * 3)
    return pl.pallas_call(
        functools.partial(_attn_kernel, cfg=cfg, blk=blk), grid_spec=grid_spec,
        out_shape=jax.ShapeDtypeStruct((cfg.tokens, cfg.attn_width), BF16),
        compiler_params=_params(("parallel", "parallel")), name="dilated_attention",
    )(slopes, proj, proj, proj)


def _conv_kernel(gb_ref, gc_ref, val_ref, w_ref, o_ref):
    u = gc_ref[...] * val_ref[...]
    row = lax.broadcasted_iota(jnp.int32, u.shape, 0)
    u1 = jnp.where(row >= 1, pltpu.roll(u, 1, 0), 0.0)
    u2 = jnp.where(row >= 2, pltpu.roll(u, 2, 0), 0.0)
    w = w_ref[...]
    z = w[0:1, :] * u2 + w[1:2, :] * u1 + w[2:3, :] * u
    o_ref[...] = (gb_ref[...] * z).astype(o_ref.dtype)


def _gated_conv(proj, conv_w, cfg):
    seq, cw, c = cfg.seq, cfg.conv_cw, cfg.conv_width
    base = 3 * cfg.attn_width // cw
    per = c // cw
    spec = lambda off: pl.BlockSpec((seq, cw), lambda b, j, off=off: (b, base + off * per + j))
    return pl.pallas_call(
        _conv_kernel, grid=(cfg.batch, per),
        in_specs=[spec(0), spec(1), spec(2), pl.BlockSpec((conv_w.shape[0], cw), lambda b, j: (0, j))],
        out_specs=pl.BlockSpec((seq, cw), lambda b, j: (b, j)),
        out_shape=jax.ShapeDtypeStruct((cfg.tokens, c), BF16),
        compiler_params=_params(("parallel", "parallel")), name="gated_conv",
    )(proj, proj, proj, conv_w)


def _ln1_router_kernel(x_ref, mix_ref, gi_ref, bi_ref, g1_ref, b1_ref, rw_ref, rb_ref,
                       h_ref, hp_ref, idx_ref, gate_ref, *, alpha, top_k):
    h0 = _layer_norm_rows(x_ref[...], gi_ref[...], bi_ref[...])
    h1 = _layer_norm_rows(alpha * h0 + mix_ref[...], g1_ref[...], b1_ref[...])
    h_ref[...] = h1
    hp_ref[...] = _pack_halves(h1)
    def split(v):
        hi = v.astype(BF16)
        return hi, (v - hi.astype(F32)).astype(BF16)

    def dot_t(a, b):
        return lax.dot_general(a, b, (((1,), (1,)), ((), ())), preferred_element_type=F32)
    w_hi, w_lo = split(rw_ref[...])
    h_hi, h_lo = split(h1)
    logits = dot_t(w_hi, h_hi) + (dot_t(w_hi, h_lo) + dot_t(w_lo, h_hi)) + rb_ref[...]
    n_e = logits.shape[0]
    eidx = lax.broadcasted_iota(jnp.int32, logits.shape, 0)
    vals, idxs = [], []
    for _ in range(top_k):
        m = jnp.max(logits, axis=0, keepdims=True)
        i = jnp.min(jnp.where(logits == m, eidx, n_e), axis=0, keepdims=True)
        vals.append(m)
        idxs.append(i)
        logits = jnp.where(eidx == i, -jnp.inf, logits)
    ex = [jnp.exp(v - vals[0]) for v in vals]
    den = functools.reduce(lambda a, b: a + b, ex)
    idx_ref[...] = jnp.concatenate(idxs, axis=0)
    gate_ref[...] = jnp.concatenate([e / den for e in ex], axis=0)


def _ln1_router(x2d, mix, gi, bi, g1, b1, router_w, router_b, cfg):
    t, d = x2d.shape
    e, k = cfg.n_experts, cfg.top_k
    tm = cfg.ln_rows
    alpha = (2 * cfg.depth) ** 0.25
    row = pl.BlockSpec((tm, d), lambda i: (i, 0))
    vec = pl.BlockSpec((1, d), lambda i: (0, 0))
    sel = pl.BlockSpec((k, tm), lambda i: (0, i))
    return pl.pallas_call(
        functools.partial(_ln1_router_kernel, alpha=alpha, top_k=k), grid=(t // tm,),
        in_specs=[row, row, vec, vec, vec, vec,
                  pl.BlockSpec((e, d), lambda i: (0, 0)), pl.BlockSpec((e, 1), lambda i: (0, 0))],
        out_specs=[row, pl.BlockSpec((tm, d // 2), lambda i: (i, 0)), sel, sel],
        out_shape=[jax.ShapeDtypeStruct((t, d), F32), jax.ShapeDtypeStruct((t, d // 2), jnp.int32),
                   jax.ShapeDtypeStruct((k, t), jnp.int32),
                   jax.ShapeDtypeStruct((k, t), F32)],
        compiler_params=_params(("parallel",)), name="ln1_router",
    )(x2d, mix, gi.reshape(1, d), bi.reshape(1, d), g1.reshape(1, d), b1.reshape(1, d),
      router_w.T, router_b.reshape(e, 1))


def _routing_plan(top_idx, cfg):
    k, t = top_idx.shape
    e, rb = cfg.n_experts, cfg.moe_rows
    a = k * t
    nb = a // rb + e
    p = nb * rb
    i32 = jnp.int32
    flat_e = top_idx.reshape(a)
    slot = jnp.arange(a, dtype=i32)
    experts = jnp.arange(e, dtype=i32)
    sorted_e, order = lax.sort((flat_e, slot), num_keys=1)
    starts = jnp.sum((flat_e[:, None] < experts[None, :]).astype(i32), axis=0)
    counts = jnp.sum((flat_e[:, None] == experts[None, :]).astype(i32), axis=0)
    padded = (counts + rb - 1) // rb * rb
    pends = jnp.cumsum(padded)
    pstarts = pends - padded
    pos_sorted = pstarts[sorted_e] + slot - starts[sorted_e]
    _, pos = lax.sort((order, pos_sorted), num_keys=1)

    rows = jnp.arange(p, dtype=i32)
    row_e = jnp.minimum(jnp.sum((rows[:, None] >= pends[None, :]).astype(i32), axis=1), e - 1)
    local = rows - pstarts[row_e]
    src = jnp.clip(starts[row_e] + local, 0, a - 1)
    row_tok = jnp.where(local < counts[row_e], order[src] % t, 0).astype(i32)

    n_blocks = pends[-1] // rb
    bi = jnp.arange(nb, dtype=i32)
    raw_e = row_e[::rb]
    block_e = jnp.where(bi < n_blocks, raw_e, raw_e[n_blocks - 1])
    first = jnp.concatenate([jnp.ones((1,), i32), (block_e[1:] != block_e[:-1]).astype(i32)])
    tile_ord = jnp.cumsum(first) - 1
    tile_e = jnp.max(jnp.where(tile_ord[None, :] == bi[:, None], block_e[None, :], -1), axis=1)
    next_e = tile_e[jnp.minimum(tile_ord + 1, nb - 1)]
    meta = jnp.stack([n_blocks, tile_ord[-1] + 1]).astype(i32)
    return (row_tok, pos.reshape(k, t).astype(i32), block_e.astype(i32), first, tile_ord.astype(i32),
            next_e.astype(i32), meta)


ROW_DMA_UNROLL = 8


def _row_copy(src_hbm, dst_vmem, sem, src_row, dst_row):
    return pltpu.make_async_copy(src_hbm.at[pl.ds(src_row, 1), :], dst_vmem.at[pl.ds(dst_row, 1), :], sem)


def _start_rows(src_hbm, dst_vmem, sem, n, src_row_of, priorities=(0, 1)):
    def body(c, carry):
        for u in range(ROW_DMA_UNROLL):
            r = c * ROW_DMA_UNROLL + u
            _row_copy(src_hbm, dst_vmem, sem, src_row_of(r), r).start(priority=priorities[u % len(priorities)])
        return carry
    lax.fori_loop(0, n // ROW_DMA_UNROLL, body, 0)


def _wait_rows(src_hbm, dst_vmem, sem, n):
    def body(c, carry):
        for u in range(ROW_DMA_UNROLL):
            _row_copy(src_hbm, dst_vmem, sem, 0, c * ROW_DMA_UNROLL + u).wait()
        return carry
    lax.fori_loop(0, n // ROW_DMA_UNROLL, body, 0)


WEIGHT_CAST_CHUNKS = 4
WEIGHT_DMA_PRIORITY = 1


class _TilePlan:
    def __init__(self, be_ref, first_ref, ord_ref, next_ref, meta_ref):
        j, i = pl.program_id(0), pl.program_id(1)
        self.j, self.i = j, i
        self.expert = be_ref[i]
        self.first = first_ref[i] == 1
        self.valid = i < meta_ref[0]
        tile = j * meta_ref[1] + ord_ref[i]
        self.slot = tile & 1
        self.is_tile0 = tile == 0
        nxt = next_ref[i]
        more_j = j + 1 < pl.num_programs(0)
        self.has_next = jnp.logical_or(nxt >= 0, more_j)
        self.next_e = jnp.where(nxt >= 0, nxt, be_ref[0])
        self.next_j = jnp.where(nxt >= 0, j, j + 1)


def _first_step_dot(x, stage, w_s):
    k = x.shape[1]
    ck = k // WEIGHT_CAST_CHUNKS
    acc = None
    for c in range(WEIGHT_CAST_CHUNKS):
        wb = stage[pl.ds(c * ck, ck), :].astype(BF16)
        w_s[pl.ds(c * ck, ck), :] = wb
        part = jnp.dot(x[:, c * ck:(c + 1) * ck], wb, preferred_element_type=F32)
        acc = part if acc is None else acc + part
    return acc


def _swiglu(g, u):
    g = jnp.minimum(g, SWIGLU_LIMIT)
    u = jnp.clip(u, -SWIGLU_LIMIT, SWIGLU_LIMIT)
    return (u + 1.0) * (g * jax.nn.sigmoid(SWIGLU_ALPHA * g))


def _gate_up_kernel(be_ref, first_ref, ord_ref, next_ref, meta_ref, *refs, tf, f, j0, gather):
    if gather:
        (tok_ref, tok_next_ref, h_hbm, w_hbm, bg_ref, bu_ref, x_ref, o_ref,
         rows, stage_g, stage_u, wg_s, wu_s, row_sem, sem) = refs
    else:
        x_ref, w_hbm, bg_ref, bu_ref, o_ref, stage_g, stage_u, wg_s, wu_s, sem = refs
    plan = _TilePlan(be_ref, first_ref, ord_ref, next_ref, meta_ref)

    if gather:
        n, half = rows.shape[1], rows.shape[2]
        slot = plan.i % 2
        row_queue = (1 - WEIGHT_DMA_PRIORITY,)

        @pl.when(plan.i == 0)
        def _():
            _start_rows(h_hbm, rows.at[0], row_sem.at[0], n, lambda r: tok_ref[0, r], row_queue)

        @pl.when(plan.i + 1 < meta_ref[0])
        def _():
            _start_rows(h_hbm, rows.at[1 - slot], row_sem.at[1 - slot], n, lambda r: tok_next_ref[0, r],
                        row_queue)

        @pl.when(plan.valid)
        def _():
            _wait_rows(h_hbm, rows.at[slot], row_sem.at[slot], n)
            words = rows[slot]
            x_ref[:, :half] = _unpack_half(words, 0).astype(x_ref.dtype)
            x_ref[:, half:] = _unpack_half(words, 1).astype(x_ref.dtype)

        @pl.when(jnp.logical_not(plan.valid))
        def _():
            x_ref[...] = jnp.zeros(x_ref.shape, x_ref.dtype)

    def tile_copies(e, j, slot):
        col = pl.multiple_of((j + j0) * tf, tf)
        return (pltpu.make_async_copy(w_hbm.at[e, :, pl.ds(col, tf)], stage_g.at[slot], sem.at[slot]),
                pltpu.make_async_copy(w_hbm.at[e, :, pl.ds(f + col, tf)], stage_u.at[slot], sem.at[slot]))

    @pl.when(plan.first)
    def _():
        @pl.when(plan.is_tile0)
        def _():
            for c in tile_copies(plan.expert, plan.j, plan.slot):
                c.start(priority=WEIGHT_DMA_PRIORITY)
        for c in tile_copies(plan.expert, plan.j, plan.slot):
            c.wait()

        @pl.when(plan.has_next)
        def _():
            for c in tile_copies(plan.next_e, plan.next_j, 1 - plan.slot):
                c.start(priority=WEIGHT_DMA_PRIORITY)
        x = x_ref[...]
        g = _first_step_dot(x, stage_g.at[plan.slot], wg_s) + bg_ref[...]
        u = _first_step_dot(x, stage_u.at[plan.slot], wu_s) + bu_ref[...]
        o_ref[...] = _swiglu(g, u).astype(o_ref.dtype)

    @pl.when(jnp.logical_and(jnp.logical_not(plan.first), plan.valid))
    def _():
        x = x_ref[...]
        g = jnp.dot(x, wg_s[...], preferred_element_type=F32) + bg_ref[...]
        u = jnp.dot(x, wu_s[...], preferred_element_type=F32) + bu_ref[...]
        o_ref[...] = _swiglu(g, u).astype(o_ref.dtype)

    @pl.when(jnp.logical_not(plan.valid))
    def _():
        o_ref[...] = jnp.zeros(o_ref.shape, o_ref.dtype)


def _gate_up(x_src, w_gu, b_gu, plan_arrays, cfg, j0, nj, row_tok=None):
    gather = row_tok is not None
    e, f, rb, tf = cfg.n_experts, cfg.d_expert, cfg.moe_rows, cfg.gu_tf
    d = cfg.d_model
    nf = f // tf
    p = row_tok.shape[0] if gather else x_src.shape[0]
    nb = p // rb
    assert d % (WEIGHT_CAST_CHUNKS * 128) == 0 and rb % ROW_DMA_UNROLL == 0
    bias = lambda off: pl.BlockSpec((None, 1, tf), lambda j, i, be, *_: (be[i], 0, off + j0 + j))
    act_spec = pl.BlockSpec((rb, tf), lambda j, i, be, *_: (i, j))
    row_spec = pl.BlockSpec((rb, d), lambda j, i, be, *_: (i, 0))
    weights = [pltpu.VMEM((2, d, tf), F32), pltpu.VMEM((2, d, tf), F32),
               pltpu.VMEM((d, tf), BF16), pltpu.VMEM((d, tf), BF16)]
    act_shape = jax.ShapeDtypeStruct((p, nj * tf), BF16)
    b3 = b_gu.reshape(e, 1, 2 * f)
    if gather:
        assert nj == 1
        tok = row_tok.reshape(nb, 1, rb)
        idx = lambda g: pl.BlockSpec((None, 1, rb), lambda j, i, *_: (g(i), 0, 0), memory_space=pltpu.SMEM)
        in_specs = [idx(lambda i: i), idx(lambda i: jnp.minimum(i + 1, nb - 1)),
                    pl.BlockSpec(memory_space=pl.ANY), pl.BlockSpec(memory_space=pl.ANY), bias(0), bias(nf)]
        out_specs = [row_spec, act_spec]
        out_shape = [jax.ShapeDtypeStruct((p, d), BF16), act_shape]
        scratch = [pltpu.VMEM((2, rb, d // 2), jnp.int32)] + weights + [pltpu.SemaphoreType.DMA((2,))] * 2
        operands = (tok, tok, x_src, w_gu, b3, b3)
    else:
        in_specs = [row_spec, pl.BlockSpec(memory_space=pl.ANY), bias(0), bias(nf)]
        out_specs, out_shape = act_spec, act_shape
        scratch = weights + [pltpu.SemaphoreType.DMA((2,))]
        operands = (x_src, w_gu, b3, b3)
    grid_spec = pltpu.PrefetchScalarGridSpec(
        num_scalar_prefetch=5, grid=(nj, nb), in_specs=in_specs, out_specs=out_specs, scratch_shapes=scratch)
    return pl.pallas_call(
        functools.partial(_gate_up_kernel, tf=tf, f=f, j0=j0, gather=gather), grid_spec=grid_spec,
        out_shape=out_shape, compiler_params=_params(("arbitrary", "arbitrary")),
        name="moe_gather_gate_up" if gather else "moe_gate_up",
    )(*plan_arrays, *operands)


def _down_kernel(be_ref, first_ref, ord_ref, next_ref, meta_ref, *refs, tn):
    *a_refs, w_hbm, b_ref, o_ref, stage, w_s, sem = refs
    plan = _TilePlan(be_ref, first_ref, ord_ref, next_ref, meta_ref)

    def act():
        return jnp.concatenate([r[...] for r in a_refs], axis=1) if len(a_refs) > 1 else a_refs[0][...]

    def tile_copy(e, j, slot):
        col = pl.multiple_of(j * tn, tn)
        return pltpu.make_async_copy(w_hbm.at[e, :, pl.ds(col, tn)], stage.at[slot], sem.at[slot])

    @pl.when(plan.first)
    def _():
        @pl.when(plan.is_tile0)
        def _():
            tile_copy(plan.expert, plan.j, plan.slot).start(priority=WEIGHT_DMA_PRIORITY)
        tile_copy(plan.expert, plan.j, plan.slot).wait()

        @pl.when(plan.has_next)
        def _():
            tile_copy(plan.next_e, plan.next_j, 1 - plan.slot).start(priority=WEIGHT_DMA_PRIORITY)
        o_ref[...] = _pack_halves(_first_step_dot(act(), stage.at[plan.slot], w_s) + b_ref[...])

    @pl.when(jnp.logical_and(jnp.logical_not(plan.first), plan.valid))
    def _():
        o_ref[...] = _pack_halves(jnp.dot(act(), w_s[...], preferred_element_type=F32) + b_ref[...])

    @pl.when(jnp.logical_not(plan.valid))
    def _():
        o_ref[...] = jnp.zeros(o_ref.shape, o_ref.dtype)


def _down(acts, w_dn, b_dn, plan_arrays, cfg):
    p = acts[0].shape[0]
    f = sum(a.shape[1] for a in acts)
    e, d, rb, tn = cfg.n_experts, cfg.d_model, cfg.moe_rows, cfg.dn_tn
    assert f == cfg.d_expert and f % (WEIGHT_CAST_CHUNKS * 128) == 0
    grid_spec = pltpu.PrefetchScalarGridSpec(
        num_scalar_prefetch=5, grid=(d // tn, p // rb),
        in_specs=[pl.BlockSpec((rb, a.shape[1]), lambda j, i, be, *_: (i, 0)) for a in acts] + [
                  pl.BlockSpec(memory_space=pl.ANY),
                  pl.BlockSpec((None, 1, tn), lambda j, i, be, *_: (be[i], 0, j))],
        out_specs=pl.BlockSpec((rb, tn // 2), lambda j, i, be, *_: (i, j)),
        scratch_shapes=[pltpu.VMEM((2, f, tn), F32), pltpu.VMEM((f, tn), BF16),
                        pltpu.SemaphoreType.DMA((2,))])
    return pl.pallas_call(
        functools.partial(_down_kernel, tn=tn), grid_spec=grid_spec,
        out_shape=jax.ShapeDtypeStruct((p, d // 2), jnp.int32),
        compiler_params=_params(("arbitrary", "arbitrary")), name="moe_down",
    )(*plan_arrays, *acts, w_dn, b_dn.reshape(e, 1, d))


def _combine_kernel(pos_ref, pos_next_ref, ys_hbm, h_ref, gate_ref, g2_ref, b2_ref, o_ref, buf, sem, *,
                    alpha, tile):
    i = pl.program_id(0)
    top_k, n = buf.shape[1], buf.shape[2]
    slot = i % 2
    hw = tile // 2

    def start(idx_ref, s):
        for k in range(top_k):
            _start_rows(ys_hbm, buf.at[s, k], sem.at[s], n, lambda t, k=k: idx_ref[k, t])

    @pl.when(i == 0)
    def _():
        start(pos_ref, 0)

    @pl.when(i + 1 < pl.num_programs(0))
    def _():
        start(pos_next_ref, 1 - slot)

    for k in range(top_k):
        _wait_rows(ys_hbm, buf.at[slot, k], sem.at[slot], n)
    gates = gate_ref[...]
    slabs = []
    for j in range(buf.shape[3] // hw):
        words = [buf[slot, k, :, j * hw:(j + 1) * hw] for k in range(top_k)]
        for half in (0, 1):
            terms = [gates[:, k:k + 1] * _unpack_half(words[k], half) for k in range(top_k)]
            slabs.append(functools.reduce(lambda a, b: a + b, terms))
    y = jnp.concatenate(slabs, axis=1)
    o_ref[...] = _layer_norm_rows(alpha * h_ref[...] + y, g2_ref[...], b2_ref[...])


def _combine(ys, pos, gates_tk, h1, g2, b2, cfg):
    t, d = h1.shape
    k, n = cfg.top_k, cfg.combine_rows
    steps = t // n
    assert n % ROW_DMA_UNROLL == 0
    alpha = (2 * cfg.depth) ** 0.25
    pos_blocks = pos.reshape(k, steps, n).transpose(1, 0, 2)
    idx = lambda f: pl.BlockSpec((None, k, n), lambda i: (f(i), 0, 0), memory_space=pltpu.SMEM)
    row = pl.BlockSpec((n, d), lambda i: (i, 0))
    vec = pl.BlockSpec((1, d), lambda i: (0, 0))
    return pl.pallas_call(
        functools.partial(_combine_kernel, alpha=alpha, tile=cfg.dn_tn), grid=(steps,),
        in_specs=[idx(lambda i: i), idx(lambda i: jnp.minimum(i + 1, steps - 1)),
                  pl.BlockSpec(memory_space=pl.ANY), row,
                  pl.BlockSpec((n, k), lambda i: (i, 0)), vec, vec],
        out_specs=row,
        out_shape=jax.ShapeDtypeStruct((t, d), F32),
        scratch_shapes=[pltpu.VMEM((2, k, n, d // 2), jnp.int32), pltpu.SemaphoreType.DMA((2,))],
        compiler_params=_params(("arbitrary",)), name="moe_combine_ln2",
    )(pos_blocks, pos_blocks, ys, h1, gates_tk, g2.reshape(1, d), b2.reshape(1, d))


def _forward(cfg, x, ln_in_g, ln_in_b, w_in, conv_w, w_out, ln1_g, ln1_b, router_w, router_b,
             w_gate_up, b_gate_up, w_down, b_down, ln2_g, ln2_b):
    assert cfg.depth == 1
    b, s, d = x.shape
    x2d = x.reshape(b * s, d)
    slopes = jnp.asarray(_alibi_slopes(cfg.n_heads))
    h0 = _ln_in(x2d, ln_in_g, ln_in_b, cfg)
    proj = _matmul([h0], w_in[0], cfg, F32, "in_proj")
    attn = _attention(proj, slopes, cfg)
    y_conv = _gated_conv(proj, conv_w[0], cfg)
    mix = _matmul([attn, y_conv], w_out[0], cfg, F32, "out_proj")
    h1, h1_packed, top_idx, gates = _ln1_router(x2d, mix, ln_in_g, ln_in_b, ln1_g[0], ln1_b[0],
                                                router_w[0], router_b[0], cfg)
    row_tok, pos, *plan_arrays = _routing_plan(top_idx, cfg)
    n_tiles = cfg.d_expert // cfg.gu_tf
    xs, act0 = _gate_up(h1_packed, w_gate_up[0], b_gate_up[0], plan_arrays, cfg, 0, 1, row_tok=row_tok)
    acts = [act0]
    if n_tiles > 1:
        acts.append(_gate_up(xs, w_gate_up[0], b_gate_up[0], plan_arrays, cfg, 1, n_tiles - 1))
    ys = _down(acts, w_down[0], b_down[0], plan_arrays, cfg)
    out = _combine(ys, pos, gates.T, h1, ln2_g[0], ln2_b[0], cfg)
    return out.reshape(b, s, d)


_CONFIG = Config(
    batch=4, seq=2048, d_model=4096, head_dim=128, n_heads=24, conv_width=1024,
    patterns=((128, 1), (512, 4), (2048, 16)), n_experts=32, top_k=4, d_expert=2048, depth=1,
    ln_rows=256, mm_tm=1024, mm_tn=512, conv_cw=256, moe_rows=256, gu_tf=512, dn_tn=2048,
    gather_rows=256, combine_rows=64)


def kernel(x, ln_in_g, ln_in_b, w_in, conv_w, w_out, ln1_g, ln1_b, router_w, router_b,
           w_gate_up, b_gate_up, w_down, b_down, ln2_g, ln2_b):
    return _forward(_CONFIG, x, ln_in_g, ln_in_b, w_in, conv_w, w_out, ln1_g, ln1_b, router_w, router_b,
                    w_gate_up, b_gate_up, w_down, b_down, ln2_g, ln2_b)
```

```python
import dataclasses
import functools
import math

import numpy as np
import jax
import jax.numpy as jnp
from jax import lax
from jax.experimental import pallas as pl
from jax.experimental.pallas import tpu as pltpu

F32 = jnp.float32
BF16 = jnp.bfloat16

LN_EPS = 1e-5
SWIGLU_LIMIT = 7.0
SWIGLU_ALPHA = 1.702
MASKED_SCORE = -1e30
V7X_VMEM_LIMIT_BYTES = 56 * 1024 * 1024


@dataclasses.dataclass(frozen=True)
class Config:
    batch: int
    seq: int
    d_model: int
    head_dim: int
    n_heads: int
    conv_width: int
    patterns: tuple
    n_experts: int
    top_k: int
    d_expert: int
    depth: int
    ln_rows: int
    mm_tm: int
    mm_tn: int
    conv_cw: int
    moe_rows: int
    gu_tf: int
    dn_tn: int
    gather_rows: int
    combine_rows: int

    @property
    def attn_width(self):
        return self.n_heads * self.head_dim

    @property
    def tokens(self):
        return self.batch * self.seq


def _params(sem, vmem=V7X_VMEM_LIMIT_BYTES):
    return pltpu.CompilerParams(dimension_semantics=sem, vmem_limit_bytes=vmem)


def _alibi_slopes(n):
    def pow2(m):
        start = 2.0 ** (-(2.0 ** -(math.log2(m) - 3)))
        return [start ** (i + 1) for i in range(m)]
    if math.log2(n).is_integer():
        s = pow2(n)
    else:
        c = 2 ** int(math.floor(math.log2(n)))
        s = pow2(c) + pow2(2 * c)[0::2][: n - c]
    return np.asarray(s, dtype=np.float32)


def _pack_halves(x):
    n = x.shape[1] // 2
    return pltpu.pack_elementwise([x[:, :n], x[:, n:]], packed_dtype=BF16)


def _unpack_half(words, half):
    return pltpu.unpack_elementwise(words, index=half, packed_dtype=BF16, unpacked_dtype=F32)


def _layer_norm_rows(x, g, b):
    mu = jnp.mean(x, axis=-1, keepdims=True)
    xc = x - mu
    var = jnp.mean(xc * xc, axis=-1, keepdims=True)
    return xc * lax.rsqrt(var + LN_EPS) * g + b


def _ln_in_kernel(x_ref, g_ref, b_ref, o_ref):
    o_ref[...] = _layer_norm_rows(x_ref[...], g_ref[...], b_ref[...]).astype(o_ref.dtype)


def _ln_in(x2d, g, b, cfg):
    t, d = x2d.shape
    tm = cfg.ln_rows
    row = pl.BlockSpec((tm, d), lambda i: (i, 0))
    vec = pl.BlockSpec((1, d), lambda i: (0, 0))
    return pl.pallas_call(
        _ln_in_kernel, grid=(t // tm,), in_specs=[row, vec, vec], out_specs=row,
        out_shape=jax.ShapeDtypeStruct((t, d), BF16),
        compiler_params=_params(("parallel",)), name="ln_in",
    )(x2d, g.reshape(1, d), b.reshape(1, d))


def _matmul_kernel(*refs):
    *xw_refs, o_ref = refs
    n_in = len(xw_refs) // 2
    acc = None
    for x_ref, w_ref in zip(xw_refs[:n_in], xw_refs[n_in:]):
        part = jnp.dot(x_ref[...], w_ref[...].astype(BF16), preferred_element_type=F32)
        acc = part if acc is None else acc + part
    o_ref[...] = acc.astype(o_ref.dtype)


def _matmul(xs, w, cfg, out_dtype, name):
    m = xs[0].shape[0]
    n = w.shape[1]
    tm, tn = cfg.mm_tm, cfg.mm_tn
    x_specs, w_specs, row0 = [], [], 0
    for x in xs:
        k = x.shape[1]
        assert row0 % k == 0
        x_specs.append(pl.BlockSpec((tm, k), lambda i, j: (i, 0)))
        w_specs.append(pl.BlockSpec((k, tn), lambda i, j, r=row0 // k: (r, j)))
        row0 += k
    assert row0 == w.shape[0]
    return pl.pallas_call(
        _matmul_kernel, grid=(m // tm, n // tn),
        in_specs=x_specs + w_specs,
        out_specs=pl.BlockSpec((tm, tn), lambda i, j: (i, j)),
        out_shape=jax.ShapeDtypeStruct((m, n), out_dtype),
        compiler_params=_params(("parallel", "parallel")), name=name,
    )(*xs, *([w] * len(xs)))


def _attn_kernel(slopes_ref, q_ref, k_ref, v_ref, o_ref, acc_ref, m_ref, l_ref, *, cfg, blk):
    seq, hd = cfg.seq, cfg.head_dim
    nblk = seq // blk
    slope = slopes_ref[pl.program_id(1)]
    scale = 1.0 / math.sqrt(hd)

    qi = lax.broadcasted_iota(jnp.int32, (blk, blk), 0)
    ki = lax.broadcasted_iota(jnp.int32, (blk, blk), 1)
    d_own = qi - ki
    own_steps = d_own.astype(F32)
    prev_steps = (d_own + blk).astype(F32)

    for g, (window, dil) in enumerate(cfg.patterns):
        sub = seq // dil
        nb = sub // blk
        neg = -slope * float(dil)

        def to_blocks(ref, f, dil=dil, sub=sub, nb=nb):
            if dil == 1:
                return f(ref[...]).reshape(nblk, blk, hd)
            return jnp.concatenate(
                [f(ref[pl.ds(r, sub, stride=dil), :]).reshape(nb, blk, hd) for r in range(dil)], axis=0)

        def from_blocks(dst, x, g=g, dil=dil, sub=sub, nb=nb):
            if dil == 1:
                dst[g] = x.reshape(seq, hd)
            else:
                for r in range(dil):
                    dst[g, pl.ds(r, sub, stride=dil), :] = x[r * nb:(r + 1) * nb].reshape(sub, hd)

        q = to_blocks(q_ref, lambda x: (x * scale).astype(BF16))
        keys = to_blocks(k_ref, lambda x: x.astype(BF16))
        vals = to_blocks(v_ref, lambda x: x.astype(BF16))
        bias = jnp.where(d_own >= 0, own_steps * neg, MASKED_SCORE)[None]
        if nb > 1:
            def with_prev(x, dil=dil, nb=nb):
                x4 = x.reshape(dil, nb, blk, hd)
                prev = jnp.concatenate([jnp.zeros((dil, 1, blk, hd), x.dtype), x4[:, :-1]], axis=1)
                return jnp.concatenate([prev.reshape(nblk, blk, hd), x], axis=1)
            keys, vals = with_prev(keys), with_prev(vals)
            n_of_block = lax.broadcasted_iota(jnp.int32, (nblk, blk, blk), 0) & (nb - 1)
            prev_ok = jnp.logical_and(d_own[None] <= 0, n_of_block > 0)
            bias_prev = jnp.where(prev_ok, (prev_steps * neg)[None], MASKED_SCORE)
            bias = jnp.concatenate([bias_prev, jnp.broadcast_to(bias, (nblk, blk, blk))], axis=2)
        s = jnp.einsum("bqd,bkd->bqk", q, keys, preferred_element_type=F32) + bias
        m = jnp.max(s, axis=-1, keepdims=True)
        p = jnp.exp(s - m).astype(BF16)
        vals1 = jnp.concatenate([vals, jnp.ones(vals.shape, BF16)], axis=-1)
        pv = jnp.einsum("bqk,bkd->bqd", p, vals1, preferred_element_type=F32)
        from_blocks(acc_ref, pv[:, :, :hd])
        from_blocks(l_ref, pv[:, :, hd:])
        from_blocks(m_ref, jnp.broadcast_to(m, (nblk, blk, hd)))

    n_win = len(cfg.patterns)
    gs = max(c for c in range(1, 5) if nblk % c == 0)

    def merge(i, carry):
        for j in range(gs):
            r = pl.ds(pl.multiple_of((i * gs + j) * blk, blk), blk)
            ms = [m_ref[g, r, :] for g in range(n_win)]
            m_all = functools.reduce(jnp.maximum, ms)
            ws = [jnp.exp(m - m_all) for m in ms]
            num = functools.reduce(lambda a, b: a + b, [w * acc_ref[g, r, :] for g, w in enumerate(ws)])
            den = functools.reduce(lambda a, b: a + b, [w * l_ref[g, r, :] for g, w in enumerate(ws)])
            o_ref[r, :] = (num / den).astype(o_ref.dtype)
        return carry
    lax.fori_loop(0, nblk // gs, merge, 0)


def _attention(proj, slopes, cfg):
    seq, hd, nh = cfg.seq, cfg.head_dim, cfg.n_heads
    blks = {w // d for w, d in cfg.patterns}
    assert len(blks) == 1, "all windows must share one block length"
    blk = blks.pop()
    for w, d in cfg.patterns:
        nb = seq // (d * blk)
        assert d & (d - 1) == 0 and nb & (nb - 1) == 0 and nb * d * blk == seq
    grid_spec = pltpu.PrefetchScalarGridSpec(
        num_scalar_prefetch=1, grid=(cfg.batch, nh),
        in_specs=[pl.BlockSpec((seq, hd), lambda b, h, s: (b, h)),
                  pl.BlockSpec((seq, hd), lambda b, h, s: (b, nh + h)),
                  pl.BlockSpec((seq, hd), lambda b, h, s: (b, 2 * nh + h))],
        out_specs=pl.BlockSpec((seq, hd), lambda b, h, s: (b, h)),
        scratch_shapes=[pltpu.VMEM((len(cfg.patterns), seq, hd), F32)] * 3)
    return pl.pallas_call(
        functools.partial(_attn_kernel, cfg=cfg, blk=blk), grid_spec=grid_spec,
        out_shape=jax.ShapeDtypeStruct((cfg.tokens, cfg.attn_width), BF16),
        compiler_params=_params(("parallel", "parallel")), name="dilated_attention",
    )(slopes, proj, proj, proj)


def _conv_kernel(gb_ref, gc_ref, val_ref, w_ref, o_ref):
    u = gc_ref[...] * val_ref[...]
    row = lax.broadcasted_iota(jnp.int32, u.shape, 0)
    u1 = jnp.where(row >= 1, pltpu.roll(u, 1, 0), 0.0)
    u2 = jnp.where(row >= 2, pltpu.roll(u, 2, 0), 0.0)
    w = w_ref[...]
    z = w[0:1, :] * u2 + w[1:2, :] * u1 + w[2:3, :] * u
    o_ref[...] = (gb_ref[...] * z).astype(o_ref.dtype)


def _gated_conv(proj, conv_w, cfg):
    seq, cw, c = cfg.seq, cfg.conv_cw, cfg.conv_width
    base = 3 * cfg.attn_width // cw
    per = c // cw
    spec = lambda off: pl.BlockSpec((seq, cw), lambda b, j, off=off: (b, base + off * per + j))
    return pl.pallas_call(
        _conv_kernel, grid=(cfg.batch, per),
        in_specs=[spec(0), spec(1), spec(2), pl.BlockSpec((conv_w.shape[0], cw), lambda b, j: (0, j))],
        out_specs=pl.BlockSpec((seq, cw), lambda b, j: (b, j)),
        out_shape=jax.ShapeDtypeStruct((cfg.tokens, c), BF16),
        compiler_params=_params(("parallel", "parallel")), name="gated_conv",
    )(proj, proj, proj, conv_w)


def _ln1_router_kernel(x_ref, mix_ref, gi_ref, bi_ref, g1_ref, b1_ref, rw_ref, rb_ref,
                       h_ref, hp_ref, idx_ref, gate_ref, *, alpha, top_k):
    h0 = _layer_norm_rows(x_ref[...], gi_ref[...], bi_ref[...])
    h1 = _layer_norm_rows(alpha * h0 + mix_ref[...], g1_ref[...], b1_ref[...])
    h_ref[...] = h1
    hp_ref[...] = _pack_halves(h1)
    def split(v):
        hi = v.astype(BF16)
        return hi, (v - hi.astype(F32)).astype(BF16)

    def dot_t(a, b):
        return lax.dot_general(a, b, (((1,), (1,)), ((), ())), preferred_element_type=F32)
    w_hi, w_lo = split(rw_ref[...])
    h_hi, h_lo = split(h1)
    logits = dot_t(w_hi, h_hi) + (dot_t(w_hi, h_lo) + dot_t(w_lo, h_hi)) + rb_ref[...]
    n_e = logits.shape[0]
    eidx = lax.broadcasted_iota(jnp.int32, logits.shape, 0)
    vals, idxs = [], []
    for _ in range(top_k):
        m = jnp.max(logits, axis=0, keepdims=True)
        i = jnp.min(jnp.where(logits == m, eidx, n_e), axis=0, keepdims=True)
        vals.append(m)
        idxs.append(i)
        logits = jnp.where(eidx == i, -jnp.inf, logits)
    ex = [jnp.exp(v - vals[0]) for v in vals]
    den = functools.reduce(lambda a, b: a + b, ex)
    idx_ref[...] = jnp.concatenate(idxs, axis=0)
    gate_ref[...] = jnp.concatenate([e / den for e in ex], axis=0)


def _ln1_router(x2d, mix, gi, bi, g1, b1, router_w, router_b, cfg):
    t, d = x2d.shape
    e, k = cfg.n_experts, cfg.top_k
    tm = cfg.ln_rows
    alpha = (2 * cfg.depth) ** 0.25
    row = pl.BlockSpec((tm, d), lambda i: (i, 0))
    vec = pl.BlockSpec((1, d), lambda i: (0, 0))
    sel = pl.BlockSpec((k, tm), lambda i: (0, i))
    return pl.pallas_call(
        functools.partial(_ln1_router_kernel, alpha=alpha, top_k=k), grid=(t // tm,),
        in_specs=[row, row, vec, vec, vec, vec,
                  pl.BlockSpec((e, d), lambda i: (0, 0)), pl.BlockSpec((e, 1), lambda i: (0, 0))],
        out_specs=[row, pl.BlockSpec((tm, d // 2), lambda i: (i, 0)), sel, sel],
        out_shape=[jax.ShapeDtypeStruct((t, d), F32), jax.ShapeDtypeStruct((t, d // 2), jnp.int32),
                   jax.ShapeDtypeStruct((k, t), jnp.int32),
                   jax.ShapeDtypeStruct((k, t), F32)],
        compiler_params=_params(("parallel",)), name="ln1_router",
    )(x2d, mix, gi.reshape(1, d), bi.reshape(1, d), g1.reshape(1, d), b1.reshape(1, d),
      router_w.T, router_b.reshape(e, 1))


def _routing_plan(top_idx, cfg):
    k, t = top_idx.shape
    e, rb = cfg.n_experts, cfg.moe_rows
    a = k * t
    nb = a // rb + e
    p = nb * rb
    i32 = jnp.int32
    flat_e = top_idx.reshape(a)
    slot = jnp.arange(a, dtype=i32)
    experts = jnp.arange(e, dtype=i32)
    sorted_e, order = lax.sort((flat_e, slot), num_keys=1)
    starts = jnp.sum((flat_e[:, None] < experts[None, :]).astype(i32), axis=0)
    counts = jnp.sum((flat_e[:, None] == experts[None, :]).astype(i32), axis=0)
    padded = (counts + rb - 1) // rb * rb
    pends = jnp.cumsum(padded)
    pstarts = pends - padded
    pos_sorted = pstarts[sorted_e] + slot - starts[sorted_e]
    _, pos = lax.sort((order, pos_sorted), num_keys=1)

    rows = jnp.arange(p, dtype=i32)
    row_e = jnp.minimum(jnp.sum((rows[:, None] >= pends[None, :]).astype(i32), axis=1), e - 1)
    local = rows - pstarts[row_e]
    src = jnp.clip(starts[row_e] + local, 0, a - 1)
    row_tok = jnp.where(local < counts[row_e], order[src] % t, 0).astype(i32)

    n_blocks = pends[-1] // rb
    bi = jnp.arange(nb, dtype=i32)
    raw_e = row_e[::rb]
    block_e = jnp.where(bi < n_blocks, raw_e, raw_e[n_blocks - 1])
    first = jnp.concatenate([jnp.ones((1,), i32), (block_e[1:] != block_e[:-1]).astype(i32)])
    tile_ord = jnp.cumsum(first) - 1
    tile_e = jnp.max(jnp.where(tile_ord[None, :] == bi[:, None], block_e[None, :], -1), axis=1)
    next_e = tile_e[jnp.minimum(tile_ord + 1, nb - 1)]
    meta = jnp.stack([n_blocks, tile_ord[-1] + 1]).astype(i32)
    return (row_tok, pos.reshape(k, t).astype(i32), block_e.astype(i32), first, tile_ord.astype(i32),
            next_e.astype(i32), meta)


ROW_DMA_UNROLL = 8


def _row_copy(src_hbm, dst_vmem, sem, src_row, dst_row):
    return pltpu.make_async_copy(src_hbm.at[pl.ds(src_row, 1), :], dst_vmem.at[pl.ds(dst_row, 1), :], sem)


def _start_rows(src_hbm, dst_vmem, sem, n, src_row_of, inline=False):
    if inline:
        for r in range(n):
            _row_copy(src_hbm, dst_vmem, sem, src_row_of(r), r).start(priority=r % 2)
        return

    def body(c, carry):
        for u in range(ROW_DMA_UNROLL):
            r = c * ROW_DMA_UNROLL + u
            _row_copy(src_hbm, dst_vmem, sem, src_row_of(r), r).start(priority=u % 2)
        return carry
    lax.fori_loop(0, n // ROW_DMA_UNROLL, body, 0)


def _wait_rows(src_hbm, dst_vmem, sem, n):
    def body(c, carry):
        for u in range(ROW_DMA_UNROLL):
            _row_copy(src_hbm, dst_vmem, sem, 0, c * ROW_DMA_UNROLL + u).wait()
        return carry
    lax.fori_loop(0, n // ROW_DMA_UNROLL, body, 0)


WEIGHT_CAST_CHUNKS = 4
WEIGHT_DMA_PRIORITY = 1


class _TilePlan:
    def __init__(self, be_ref, first_ref, ord_ref, next_ref, meta_ref):
        j, i = pl.program_id(0), pl.program_id(1)
        self.j, self.i = j, i
        self.expert = be_ref[i]
        self.first = first_ref[i] == 1
        self.valid = i < meta_ref[0]
        tile = j * meta_ref[1] + ord_ref[i]
        self.slot = tile & 1
        self.is_tile0 = tile == 0
        nxt = next_ref[i]
        more_j = j + 1 < pl.num_programs(0)
        self.has_next = jnp.logical_or(nxt >= 0, more_j)
        self.next_e = jnp.where(nxt >= 0, nxt, be_ref[0])
        self.next_j = jnp.where(nxt >= 0, j, j + 1)


def _first_step_dot(x, stage, w_s):
    k = x.shape[1]
    ck = k // WEIGHT_CAST_CHUNKS
    acc = None
    for c in range(WEIGHT_CAST_CHUNKS):
        wb = stage[pl.ds(c * ck, ck), :].astype(BF16)
        w_s[pl.ds(c * ck, ck), :] = wb
        part = jnp.dot(x[:, c * ck:(c + 1) * ck], wb, preferred_element_type=F32)
        acc = part if acc is None else acc + part
    return acc


def _swiglu(g, u):
    g = jnp.minimum(g, SWIGLU_LIMIT)
    u = jnp.clip(u, -SWIGLU_LIMIT, SWIGLU_LIMIT)
    return (u + 1.0) * (g * jax.nn.sigmoid(SWIGLU_ALPHA * g))


def _gate_up_kernel(be_ref, first_ref, ord_ref, next_ref, meta_ref, *refs, tf, f, j0, gather):
    if gather:
        (tok_ref, tok_next_ref, h_hbm, w_hbm, bg_ref, bu_ref, x_ref, o_ref,
         rows, stage_g, stage_u, wg_s, wu_s, row_sem, sem) = refs
    else:
        x_ref, w_hbm, bg_ref, bu_ref, o_ref, stage_g, stage_u, wg_s, wu_s, sem = refs
    plan = _TilePlan(be_ref, first_ref, ord_ref, next_ref, meta_ref)

    if gather:
        n, half = rows.shape[1], rows.shape[2]
        slot = plan.i % 2
        last = pl.num_programs(1) - 1

        @pl.when(plan.i == 0)
        def _():
            _start_rows(h_hbm, rows.at[0], row_sem.at[0], n, lambda r: tok_ref[0, r])

        _wait_rows(h_hbm, rows.at[slot], row_sem.at[slot], n)
        words = rows[slot]
        x_ref[:, :half] = _unpack_half(words, 0).astype(x_ref.dtype)
        x_ref[:, half:] = _unpack_half(words, 1).astype(x_ref.dtype)

    def fetch_next_rows(inline):
        if gather:
            _start_rows(h_hbm, rows.at[1 - slot], row_sem.at[1 - slot], n, lambda r: tok_next_ref[0, r],
                        inline=inline)

    def tile_copies(e, j, slot):
        col = pl.multiple_of((j + j0) * tf, tf)
        return (pltpu.make_async_copy(w_hbm.at[e, :, pl.ds(col, tf)], stage_g.at[slot], sem.at[slot]),
                pltpu.make_async_copy(w_hbm.at[e, :, pl.ds(f + col, tf)], stage_u.at[slot], sem.at[slot]))

    @pl.when(plan.first)
    def _():
        @pl.when(plan.is_tile0)
        def _():
            for c in tile_copies(plan.expert, plan.j, plan.slot):
                c.start(priority=WEIGHT_DMA_PRIORITY)
        for c in tile_copies(plan.expert, plan.j, plan.slot):
            c.wait()

        @pl.when(plan.has_next)
        def _():
            for c in tile_copies(plan.next_e, plan.next_j, 1 - plan.slot):
                c.start(priority=WEIGHT_DMA_PRIORITY)
        fetch_next_rows(inline=True)
        x = x_ref[...]
        g = _first_step_dot(x, stage_g.at[plan.slot], wg_s) + bg_ref[...]
        u = _first_step_dot(x, stage_u.at[plan.slot], wu_s) + bu_ref[...]
        o_ref[...] = _swiglu(g, u).astype(o_ref.dtype)

    @pl.when(jnp.logical_and(jnp.logical_not(plan.first), plan.valid))
    def _():
        fetch_next_rows(inline=True)
        x = x_ref[...]
        g = jnp.dot(x, wg_s[...], preferred_element_type=F32) + bg_ref[...]
        u = jnp.dot(x, wu_s[...], preferred_element_type=F32) + bu_ref[...]
        o_ref[...] = _swiglu(g, u).astype(o_ref.dtype)

    @pl.when(jnp.logical_not(plan.valid))
    def _():
        fetch_next_rows(inline=False)
        o_ref[...] = jnp.zeros(o_ref.shape, o_ref.dtype)

    if gather:
        @pl.when(plan.i == last)
        def _():
            _wait_rows(h_hbm, rows.at[1 - slot], row_sem.at[1 - slot], n)


def _gate_up(x_src, w_gu, b_gu, plan_arrays, cfg, j0, nj, row_tok=None):
    gather = row_tok is not None
    e, f, rb, tf = cfg.n_experts, cfg.d_expert, cfg.moe_rows, cfg.gu_tf
    d = cfg.d_model
    nf = f // tf
    p = row_tok.shape[0] if gather else x_src.shape[0]
    nb = p // rb
    assert d % (WEIGHT_CAST_CHUNKS * 128) == 0 and rb % ROW_DMA_UNROLL == 0
    bias = lambda off: pl.BlockSpec((None, 1, tf), lambda j, i, be, *_: (be[i], 0, off + j0 + j))
    act_spec = pl.BlockSpec((rb, tf), lambda j, i, be, *_: (i, j))
    row_spec = pl.BlockSpec((rb, d), lambda j, i, be, *_: (i, 0))
    weights = [pltpu.VMEM((2, d, tf), F32), pltpu.VMEM((2, d, tf), F32),
               pltpu.VMEM((d, tf), BF16), pltpu.VMEM((d, tf), BF16)]
    act_shape = jax.ShapeDtypeStruct((p, nj * tf), BF16)
    b3 = b_gu.reshape(e, 1, 2 * f)
    if gather:
        assert nj == 1
        tok = row_tok.reshape(nb, 1, rb)
        idx = lambda g: pl.BlockSpec((None, 1, rb), lambda j, i, *_: (g(i), 0, 0), memory_space=pltpu.SMEM)
        in_specs = [idx(lambda i: i), idx(lambda i: jnp.minimum(i + 1, nb - 1)),
                    pl.BlockSpec(memory_space=pl.ANY), pl.BlockSpec(memory_space=pl.ANY), bias(0), bias(nf)]
        out_specs = [row_spec, act_spec]
        out_shape = [jax.ShapeDtypeStruct((p, d), BF16), act_shape]
        scratch = [pltpu.VMEM((2, rb, d // 2), jnp.int32)] + weights + [pltpu.SemaphoreType.DMA((2,))] * 2
        operands = (tok, tok, x_src, w_gu, b3, b3)
    else:
        in_specs = [row_spec, pl.BlockSpec(memory_space=pl.ANY), bias(0), bias(nf)]
        out_specs, out_shape = act_spec, act_shape
        scratch = weights + [pltpu.SemaphoreType.DMA((2,))]
        operands = (x_src, w_gu, b3, b3)
    grid_spec = pltpu.PrefetchScalarGridSpec(
        num_scalar_prefetch=5, grid=(nj, nb), in_specs=in_specs, out_specs=out_specs, scratch_shapes=scratch)
    return pl.pallas_call(
        functools.partial(_gate_up_kernel, tf=tf, f=f, j0=j0, gather=gather), grid_spec=grid_spec,
        out_shape=out_shape, compiler_params=_params(("arbitrary", "arbitrary")),
        name="moe_gather_gate_up" if gather else "moe_gate_up",
    )(*plan_arrays, *operands)


def _down_kernel(be_ref, first_ref, ord_ref, next_ref, meta_ref, *refs, tn):
    *a_refs, w_hbm, b_ref, o_ref, stage, w_s, sem = refs
    plan = _TilePlan(be_ref, first_ref, ord_ref, next_ref, meta_ref)

    def act():
        return jnp.concatenate([r[...] for r in a_refs], axis=1) if len(a_refs) > 1 else a_refs[0][...]

    def tile_copy(e, j, slot):
        col = pl.multiple_of(j * tn, tn)
        return pltpu.make_async_copy(w_hbm.at[e, :, pl.ds(col, tn)], stage.at[slot], sem.at[slot])

    @pl.when(plan.first)
    def _():
        @pl.when(plan.is_tile0)
        def _():
            tile_copy(plan.expert, plan.j, plan.slot).start(priority=WEIGHT_DMA_PRIORITY)
        tile_copy(plan.expert, plan.j, plan.slot).wait()

        @pl.when(plan.has_next)
        def _():
            tile_copy(plan.next_e, plan.next_j, 1 - plan.slot).start(priority=WEIGHT_DMA_PRIORITY)
        o_ref[...] = _pack_halves(_first_step_dot(act(), stage.at[plan.slot], w_s) + b_ref[...])

    @pl.when(jnp.logical_and(jnp.logical_not(plan.first), plan.valid))
    def _():
        o_ref[...] = _pack_halves(jnp.dot(act(), w_s[...], preferred_element_type=F32) + b_ref[...])

    @pl.when(jnp.logical_not(plan.valid))
    def _():
        o_ref[...] = jnp.zeros(o_ref.shape, o_ref.dtype)


def _down(acts, w_dn, b_dn, plan_arrays, cfg):
    p = acts[0].shape[0]
    f = sum(a.shape[1] for a in acts)
    e, d, rb, tn = cfg.n_experts, cfg.d_model, cfg.moe_rows, cfg.dn_tn
    assert f == cfg.d_expert and f % (WEIGHT_CAST_CHUNKS * 128) == 0
    grid_spec = pltpu.PrefetchScalarGridSpec(
        num_scalar_prefetch=5, grid=(d // tn, p // rb),
        in_specs=[pl.BlockSpec((rb, a.shape[1]), lambda j, i, be, *_: (i, 0)) for a in acts] + [
                  pl.BlockSpec(memory_space=pl.ANY),
                  pl.BlockSpec((None, 1, tn), lambda j, i, be, *_: (be[i], 0, j))],
        out_specs=pl.BlockSpec((rb, tn // 2), lambda j, i, be, *_: (i, j)),
        scratch_shapes=[pltpu.VMEM((2, f, tn), F32), pltpu.VMEM((f, tn), BF16),
                        pltpu.SemaphoreType.DMA((2,))])
    return pl.pallas_call(
        functools.partial(_down_kernel, tn=tn), grid_spec=grid_spec,
        out_shape=jax.ShapeDtypeStruct((p, d // 2), jnp.int32),
        compiler_params=_params(("arbitrary", "arbitrary")), name="moe_down",
    )(*plan_arrays, *acts, w_dn, b_dn.reshape(e, 1, d))


def _combine_kernel(pos_ref, pos_next_ref, ys_hbm, h_ref, gate_ref, g2_ref, b2_ref, o_ref, buf, sem, *,
                    alpha, tile):
    i = pl.program_id(0)
    top_k, n = buf.shape[1], buf.shape[2]
    slot = i % 2
    hw = tile // 2

    def start(idx_ref, s, inline):
        for k in range(top_k):
            _start_rows(ys_hbm, buf.at[s, k], sem.at[s], n, lambda t, k=k: idx_ref[k, t], inline=inline)

    def wait(s):
        for k in range(top_k):
            _wait_rows(ys_hbm, buf.at[s, k], sem.at[s], n)

    @pl.when(i == 0)
    def _():
        start(pos_ref, 0, inline=False)

    wait(slot)
    start(pos_next_ref, 1 - slot, inline=True)
    gates = gate_ref[...]
    slabs = []
    for j in range(buf.shape[3] // hw):
        words = [buf[slot, k, :, j * hw:(j + 1) * hw] for k in range(top_k)]
        for half in (0, 1):
            terms = [gates[:, k:k + 1] * _unpack_half(words[k], half) for k in range(top_k)]
            slabs.append(functools.reduce(lambda a, b: a + b, terms))
    y = jnp.concatenate(slabs, axis=1)
    o_ref[...] = _layer_norm_rows(alpha * h_ref[...] + y, g2_ref[...], b2_ref[...])

    @pl.when(i == pl.num_programs(0) - 1)
    def _():
        wait(1 - slot)


def _combine(ys, pos, gates_tk, h1, g2, b2, cfg):
    t, d = h1.shape
    k, n = cfg.top_k, cfg.combine_rows
    steps = t // n
    assert n % ROW_DMA_UNROLL == 0
    alpha = (2 * cfg.depth) ** 0.25
    pos_blocks = pos.reshape(k, steps, n).transpose(1, 0, 2)
    idx = lambda f: pl.BlockSpec((None, k, n), lambda i: (f(i), 0, 0), memory_space=pltpu.SMEM)
    row = pl.BlockSpec((n, d), lambda i: (i, 0))
    vec = pl.BlockSpec((1, d), lambda i: (0, 0))
    return pl.pallas_call(
        functools.partial(_combine_kernel, alpha=alpha, tile=cfg.dn_tn), grid=(steps,),
        in_specs=[idx(lambda i: i), idx(lambda i: jnp.minimum(i + 1, steps - 1)),
                  pl.BlockSpec(memory_space=pl.ANY), row,
                  pl.BlockSpec((n, k), lambda i: (i, 0)), vec, vec],
        out_specs=row,
        out_shape=jax.ShapeDtypeStruct((t, d), F32),
        scratch_shapes=[pltpu.VMEM((2, k, n, d // 2), jnp.int32), pltpu.SemaphoreType.DMA((2,))],
        compiler_params=_params(("arbitrary",)), name="moe_combine_ln2",
    )(pos_blocks, pos_blocks, ys, h1, gates_tk, g2.reshape(1, d), b2.reshape(1, d))


def _forward(cfg, x, ln_in_g, ln_in_b, w_in, conv_w, w_out, ln1_g, ln1_b, router_w, router_b,
             w_gate_up, b_gate_up, w_down, b_down, ln2_g, ln2_b):
    assert cfg.depth == 1
    b, s, d = x.shape
    x2d = x.reshape(b * s, d)
    slopes = jnp.asarray(_alibi_slopes(cfg.n_heads))
    h0 = _ln_in(x2d, ln_in_g, ln_in_b, cfg)
    proj = _matmul([h0], w_in[0], cfg, F32, "in_proj")
    attn = _attention(proj, slopes, cfg)
    y_conv = _gated_conv(proj, conv_w[0], cfg)
    mix = _matmul([attn, y_conv], w_out[0], cfg, F32, "out_proj")
    h1, h1_packed, top_idx, gates = _ln1_router(x2d, mix, ln_in_g, ln_in_b, ln1_g[0], ln1_b[0],
                                                router_w[0], router_b[0], cfg)
    row_tok, pos, *plan_arrays = _routing_plan(top_idx, cfg)
    n_tiles = cfg.d_expert // cfg.gu_tf
    xs, act0 = _gate_up(h1_packed, w_gate_up[0], b_gate_up[0], plan_arrays, cfg, 0, 1, row_tok=row_tok)
    acts = [act0]
    if n_tiles > 1:
        acts.append(_gate_up(xs, w_gate_up[0], b_gate_up[0], plan_arrays, cfg, 1, n_tiles - 1))
    ys = _down(acts, w_down[0], b_down[0], plan_arrays, cfg)
    out = _combine(ys, pos, gates.T, h1, ln2_g[0], ln2_b[0], cfg)
    return out.reshape(b, s, d)


_CONFIG = Config(
    batch=4, seq=2048, d_model=4096, head_dim=128, n_heads=24, conv_width=1024,
    patterns=((128, 1), (512, 4), (2048, 16)), n_experts=32, top_k=4, d_expert=2048, depth=1,
    ln_rows=256, mm_tm=1024, mm_tn=512, conv_cw=256, moe_rows=256, gu_tf=512, dn_tn=2048,
    gather_rows=256, combine_rows=64)


def kernel(x, ln_in_g, ln_in_b, w_in, conv_w, w_out, ln1_g, ln1_b, router_w, router_b,
           w_gate_up, b_gate_up, w_down, b_down, ln2_g, ln2_b):
    return _forward(_CONFIG, x, ln_in_g, ln_in_b, w_in, conv_w, w_out, ln1_g, ln1_b, router_w, router_b,
                    w_gate_up, b_gate_up, w_down, b_down, ln2_g, ln2_b)
```

```python
import dataclasses
import functools
import math

import numpy as np
import jax
import jax.numpy as jnp
from jax import lax
from jax.experimental import pallas as pl
from jax.experimental.pallas import tpu as pltpu

F32 = jnp.float32
BF16 = jnp.bfloat16

LN_EPS = 1e-5
SWIGLU_LIMIT = 7.0
SWIGLU_ALPHA = 1.702
MASKED_SCORE = -1e30
V7X_VMEM_LIMIT_BYTES = 56 * 1024 * 1024


@dataclasses.dataclass(frozen=True)
class Config:
    batch: int
    seq: int
    d_model: int
    head_dim: int
    n_heads: int
    conv_width: int
    patterns: tuple
    n_experts: int
    top_k: int
    d_expert: int
    depth: int
    ln_rows: int
    mm_tm: int
    mm_tn: int
    conv_cw: int
    moe_rows: int
    gu_tf: int
    dn_tn: int
    gather_rows: int
    combine_rows: int

    @property
    def attn_width(self):
        return self.n_heads * self.head_dim

    @property
    def tokens(self):
        return self.batch * self.seq


def _params(sem, vmem=V7X_VMEM_LIMIT_BYTES):
    return pltpu.CompilerParams(dimension_semantics=sem, vmem_limit_bytes=vmem)


def _alibi_slopes(n):
    def pow2(m):
        start = 2.0 ** (-(2.0 ** -(math.log2(m) - 3)))
        return [start ** (i + 1) for i in range(m)]
    if math.log2(n).is_integer():
        s = pow2(n)
    else:
        c = 2 ** int(math.floor(math.log2(n)))
        s = pow2(c) + pow2(2 * c)[0::2][: n - c]
    return np.asarray(s, dtype=np.float32)


def _pack_halves(x):
    n = x.shape[1] // 2
    return pltpu.pack_elementwise([x[:, :n], x[:, n:]], packed_dtype=BF16)


def _unpack_half(words, half):
    return pltpu.unpack_elementwise(words, index=half, packed_dtype=BF16, unpacked_dtype=F32)


def _layer_norm_rows(x, g, b):
    mu = jnp.mean(x, axis=-1, keepdims=True)
    xc = x - mu
    var = jnp.mean(xc * xc, axis=-1, keepdims=True)
    return xc * lax.rsqrt(var + LN_EPS) * g + b


def _ln_in_kernel(x_ref, g_ref, b_ref, o_ref):
    o_ref[...] = _layer_norm_rows(x_ref[...], g_ref[...], b_ref[...]).astype(o_ref.dtype)


def _ln_in(x2d, g, b, cfg):
    t, d = x2d.shape
    tm = cfg.ln_rows
    row = pl.BlockSpec((tm, d), lambda i: (i, 0))
    vec = pl.BlockSpec((1, d), lambda i: (0, 0))
    return pl.pallas_call(
        _ln_in_kernel, grid=(t // tm,), in_specs=[row, vec, vec], out_specs=row,
        out_shape=jax.ShapeDtypeStruct((t, d), BF16),
        compiler_params=_params(("parallel",)), name="ln_in",
    )(x2d, g.reshape(1, d), b.reshape(1, d))


def _matmul_kernel(*refs):
    *xw_refs, o_ref = refs
    n_in = len(xw_refs) // 2
    acc = None
    for x_ref, w_ref in zip(xw_refs[:n_in], xw_refs[n_in:]):
        part = jnp.dot(x_ref[...], w_ref[...].astype(BF16), preferred_element_type=F32)
        acc = part if acc is None else acc + part
    o_ref[...] = acc.astype(o_ref.dtype)


def _matmul(xs, w, cfg, out_dtype, name):
    m = xs[0].shape[0]
    n = w.shape[1]
    tm, tn = cfg.mm_tm, cfg.mm_tn
    x_specs, w_specs, row0 = [], [], 0
    for x in xs:
        k = x.shape[1]
        assert row0 % k == 0
        x_specs.append(pl.BlockSpec((tm, k), lambda i, j: (i, 0)))
        w_specs.append(pl.BlockSpec((k, tn), lambda i, j, r=row0 // k: (r, j)))
        row0 += k
    assert row0 == w.shape[0]
    return pl.pallas_call(
        _matmul_kernel, grid=(m // tm, n // tn),
        in_specs=x_specs + w_specs,
        out_specs=pl.BlockSpec((tm, tn), lambda i, j: (i, j)),
        out_shape=jax.ShapeDtypeStruct((m, n), out_dtype),
        compiler_params=_params(("parallel", "parallel")), name=name,
    )(*xs, *([w] * len(xs)))


def _attn_kernel(slopes_ref, q_ref, k_ref, v_ref, o_ref, acc_ref, m_ref, l_ref, *, cfg, blk):
    seq, hd = cfg.seq, cfg.head_dim
    nblk = seq // blk
    slope = slopes_ref[pl.program_id(1)]
    scale = 1.0 / math.sqrt(hd)

    qi = lax.broadcasted_iota(jnp.int32, (blk, blk), 0)
    ki = lax.broadcasted_iota(jnp.int32, (blk, blk), 1)
    d_own = qi - ki
    own_steps = d_own.astype(F32)
    prev_steps = (d_own + blk).astype(F32)

    for g, (window, dil) in enumerate(cfg.patterns):
        sub = seq // dil
        nb = sub // blk
        neg = -slope * float(dil)

        def to_blocks(ref, f, dil=dil, sub=sub, nb=nb):
            if dil == 1:
                return f(ref[...]).reshape(nblk, blk, hd)
            return jnp.concatenate(
                [f(ref[pl.ds(r, sub, stride=dil), :]).reshape(nb, blk, hd) for r in range(dil)], axis=0)

        def from_blocks(dst, x, g=g, dil=dil, sub=sub, nb=nb):
            if dil == 1:
                dst[g] = x.reshape(seq, hd)
            else:
                for r in range(dil):
                    dst[g, pl.ds(r, sub, stride=dil), :] = x[r * nb:(r + 1) * nb].reshape(sub, hd)

        q = to_blocks(q_ref, lambda x: (x * scale).astype(BF16))
        keys = to_blocks(k_ref, lambda x: x.astype(BF16))
        vals = to_blocks(v_ref, lambda x: x.astype(BF16))
        bias = jnp.where(d_own >= 0, own_steps * neg, MASKED_SCORE)[None]
        if nb > 1:
            def with_prev(x, dil=dil, nb=nb):
                x4 = x.reshape(dil, nb, blk, hd)
                prev = jnp.concatenate([jnp.zeros((dil, 1, blk, hd), x.dtype), x4[:, :-1]], axis=1)
                return jnp.concatenate([prev.reshape(nblk, blk, hd), x], axis=1)
            keys, vals = with_prev(keys), with_prev(vals)
            n_of_block = lax.broadcasted_iota(jnp.int32, (nblk, blk, blk), 0) & (nb - 1)
            prev_ok = jnp.logical_and(d_own[None] <= 0, n_of_block > 0)
            bias_prev = jnp.where(prev_ok, (prev_steps * neg)[None], MASKED_SCORE)
            bias = jnp.concatenate([bias_prev, jnp.broadcast_to(bias, (nblk, blk, blk))], axis=2)
        s = jnp.einsum("bqd,bkd->bqk", q, keys, preferred_element_type=F32) + bias
        m = jnp.max(s, axis=-1, keepdims=True)
        p = jnp.exp(s - m).astype(BF16)
        vals1 = jnp.concatenate([vals, jnp.ones(vals.shape, BF16)], axis=-1)
        pv = jnp.einsum("bqk,bkd->bqd", p, vals1, preferred_element_type=F32)
        from_blocks(acc_ref, pv[:, :, :hd])
        from_blocks(l_ref, pv[:, :, hd:])
        from_blocks(m_ref, jnp.broadcast_to(m, (nblk, blk, hd)))

    n_win = len(cfg.patterns)
    gs = max(c for c in range(1, 5) if nblk % c == 0)

    def merge(i, carry):
        for j in range(gs):
            r = pl.ds(pl.multiple_of((i * gs + j) * blk, blk), blk)
            ms = [m_ref[g, r, :] for g in range(n_win)]
            m_all = functools.reduce(jnp.maximum, ms)
            ws = [jnp.exp(m - m_all) for m in ms]
            num = functools.reduce(lambda a, b: a + b, [w * acc_ref[g, r, :] for g, w in enumerate(ws)])
            den = functools.reduce(lambda a, b: a + b, [w * l_ref[g, r, :] for g, w in enumerate(ws)])
            o_ref[r, :] = (num / den).astype(o_ref.dtype)
        return carry
    lax.fori_loop(0, nblk // gs, merge, 0)


def _attention(proj, slopes, cfg):
    seq, hd, nh = cfg.seq, cfg.head_dim, cfg.n_heads
    blks = {w // d for w, d in cfg.patterns}
    assert len(blks) == 1, "all windows must share one block length"
    blk = blks.pop()
    for w, d in cfg.patterns:
        nb = seq // (d * blk)
        assert d & (d - 1) == 0 and nb & (nb - 1) == 0 and nb * d * blk == seq
    grid_spec = pltpu.PrefetchScalarGridSpec(
        num_scalar_prefetch=1, grid=(cfg.batch, nh),
        in_specs=[pl.BlockSpec((seq, hd), lambda b, h, s: (b, h)),
                  pl.BlockSpec((seq, hd), lambda b, h, s: (b, nh + h)),
                  pl.BlockSpec((seq, hd), lambda b, h, s: (b, 2 * nh + h))],
        out_specs=pl.BlockSpec((seq, hd), lambda b, h, s: (b, h)),
        scratch_shapes=[pltpu.VMEM((len(cfg.patterns), seq, hd), F32)] * 3)
    return pl.pallas_call(
        functools.partial(_attn_kernel, cfg=cfg, blk=blk), grid_spec=grid_spec,
        out_shape=jax.ShapeDtypeStruct((cfg.tokens, cfg.attn_width), BF16),
        compiler_params=_params(("parallel", "parallel")), name="dilated_attention",
    )(slopes, proj, proj, proj)


def _conv_kernel(gb_ref, gc_ref, val_ref, w_ref, o_ref):
    u = gc_ref[...] * val_ref[...]
    row = lax.broadcasted_iota(jnp.int32, u.shape, 0)
    u1 = jnp.where(row >= 1, pltpu.roll(u, 1, 0), 0.0)
    u2 = jnp.where(row >= 2, pltpu.roll(u, 2, 0), 0.0)
    w = w_ref[...]
    z = w[0:1, :] * u2 + w[1:2, :] * u1 + w[2:3, :] * u
    o_ref[...] = (gb_ref[...] * z).astype(o_ref.dtype)


def _gated_conv(proj, conv_w, cfg):
    seq, cw, c = cfg.seq, cfg.conv_cw, cfg.conv_width
    base = 3 * cfg.attn_width // cw
    per = c // cw
    spec = lambda off: pl.BlockSpec((seq, cw), lambda b, j, off=off: (b, base + off * per + j))
    return pl.pallas_call(
        _conv_kernel, grid=(cfg.batch, per),
        in_specs=[spec(0), spec(1), spec(2), pl.BlockSpec((conv_w.shape[0], cw), lambda b, j: (0, j))],
        out_specs=pl.BlockSpec((seq, cw), lambda b, j: (b, j)),
        out_shape=jax.ShapeDtypeStruct((cfg.tokens, c), BF16),
        compiler_params=_params(("parallel", "parallel")), name="gated_conv",
    )(proj, proj, proj, conv_w)


def _ln1_router_kernel(x_ref, mix_ref, gi_ref, bi_ref, g1_ref, b1_ref, rw_ref, rb_ref,
                       h_ref, hp_ref, idx_ref, gate_ref, *, alpha, top_k):
    h0 = _layer_norm_rows(x_ref[...], gi_ref[...], bi_ref[...])
    h1 = _layer_norm_rows(alpha * h0 + mix_ref[...], g1_ref[...], b1_ref[...])
    h_ref[...] = h1
    hp_ref[...] = _pack_halves(h1)
    def split(v):
        hi = v.astype(BF16)
        return hi, (v - hi.astype(F32)).astype(BF16)

    def dot_t(a, b):
        return lax.dot_general(a, b, (((1,), (1,)), ((), ())), preferred_element_type=F32)
    w_hi, w_lo = split(rw_ref[...])
    h_hi, h_lo = split(h1)
    logits = dot_t(w_hi, h_hi) + (dot_t(w_hi, h_lo) + dot_t(w_lo, h_hi)) + rb_ref[...]
    n_e = logits.shape[0]
    eidx = lax.broadcasted_iota(jnp.int32, logits.shape, 0)
    vals, idxs = [], []
    for _ in range(top_k):
        m = jnp.max(logits, axis=0, keepdims=True)
        i = jnp.min(jnp.where(logits == m, eidx, n_e), axis=0, keepdims=True)
        vals.append(m)
        idxs.append(i)
        logits = jnp.where(eidx == i, -jnp.inf, logits)
    ex = [jnp.exp(v - vals[0]) for v in vals]
    den = functools.reduce(lambda a, b: a + b, ex)
    idx_ref[...] = jnp.concatenate(idxs, axis=0)
    gate_ref[...] = jnp.concatenate([e / den for e in ex], axis=0)


def _ln1_router(x2d, mix, gi, bi, g1, b1, router_w, router_b, cfg):
    t, d = x2d.shape
    e, k = cfg.n_experts, cfg.top_k
    tm = cfg.ln_rows
    alpha = (2 * cfg.depth) ** 0.25
    row = pl.BlockSpec((tm, d), lambda i: (i, 0))
    vec = pl.BlockSpec((1, d), lambda i: (0, 0))
    sel = pl.BlockSpec((k, tm), lambda i: (0, i))
    return pl.pallas_call(
        functools.partial(_ln1_router_kernel, alpha=alpha, top_k=k), grid=(t // tm,),
        in_specs=[row, row, vec, vec, vec, vec,
                  pl.BlockSpec((e, d), lambda i: (0, 0)), pl.BlockSpec((e, 1), lambda i: (0, 0))],
        out_specs=[row, pl.BlockSpec((tm, d // 2), lambda i: (i, 0)), sel, sel],
        out_shape=[jax.ShapeDtypeStruct((t, d), F32), jax.ShapeDtypeStruct((t, d // 2), jnp.int32),
                   jax.ShapeDtypeStruct((k, t), jnp.int32),
                   jax.ShapeDtypeStruct((k, t), F32)],
        compiler_params=_params(("parallel",)), name="ln1_router",
    )(x2d, mix, gi.reshape(1, d), bi.reshape(1, d), g1.reshape(1, d), b1.reshape(1, d),
      router_w.T, router_b.reshape(e, 1))


def _routing_plan(top_idx, cfg):
    k, t = top_idx.shape
    e, rb = cfg.n_experts, cfg.moe_rows
    a = k * t
    nb = a // rb + e
    p = nb * rb
    i32 = jnp.int32
    flat_e = top_idx.reshape(a)
    slot = jnp.arange(a, dtype=i32)
    experts = jnp.arange(e, dtype=i32)
    sorted_e, order = lax.sort((flat_e, slot), num_keys=1)
    starts = jnp.sum((flat_e[:, None] < experts[None, :]).astype(i32), axis=0)
    counts = jnp.sum((flat_e[:, None] == experts[None, :]).astype(i32), axis=0)
    padded = (counts + rb - 1) // rb * rb
    pends = jnp.cumsum(padded)
    pstarts = pends - padded
    pos_sorted = pstarts[sorted_e] + slot - starts[sorted_e]
    _, pos = lax.sort((order, pos_sorted), num_keys=1)

    rows = jnp.arange(p, dtype=i32)
    row_e = jnp.minimum(jnp.sum((rows[:, None] >= pends[None, :]).astype(i32), axis=1), e - 1)
    local = rows - pstarts[row_e]
    src = jnp.clip(starts[row_e] + local, 0, a - 1)
    row_tok = jnp.where(local < counts[row_e], order[src] % t, 0).astype(i32)

    n_blocks = pends[-1] // rb
    bi = jnp.arange(nb, dtype=i32)
    raw_e = row_e[::rb]
    block_e = jnp.where(bi < n_blocks, raw_e, raw_e[n_blocks - 1])
    first = jnp.concatenate([jnp.ones((1,), i32), (block_e[1:] != block_e[:-1]).astype(i32)])
    tile_ord = jnp.cumsum(first) - 1
    tile_e = jnp.max(jnp.where(tile_ord[None, :] == bi[:, None], block_e[None, :], -1), axis=1)
    next_e = tile_e[jnp.minimum(tile_ord + 1, nb - 1)]
    meta = jnp.stack([n_blocks, tile_ord[-1] + 1]).astype(i32)
    return (row_tok, pos.reshape(k, t).astype(i32), block_e.astype(i32), first, tile_ord.astype(i32),
            next_e.astype(i32), meta)


ROW_DMA_UNROLL = 8


def _row_copy(src_hbm, dst_vmem, sem, src_row, dst_row):
    return pltpu.make_async_copy(src_hbm.at[pl.ds(src_row, 1), :], dst_vmem.at[pl.ds(dst_row, 1), :], sem)


def _start_rows(src_hbm, dst_vmem, sem, n, src_row_of, inline=False):
    if inline:
        for r in range(n):
            _row_copy(src_hbm, dst_vmem, sem, src_row_of(r), r).start(priority=r % 2)
        return

    def body(c, carry):
        for u in range(ROW_DMA_UNROLL):
            r = c * ROW_DMA_UNROLL + u
            _row_copy(src_hbm, dst_vmem, sem, src_row_of(r), r).start(priority=u % 2)
        return carry
    lax.fori_loop(0, n // ROW_DMA_UNROLL, body, 0)


def _wait_rows(src_hbm, dst_vmem, sem, n):
    def body(c, carry):
        for u in range(ROW_DMA_UNROLL):
            _row_copy(src_hbm, dst_vmem, sem, 0, c * ROW_DMA_UNROLL + u).wait()
        return carry
    lax.fori_loop(0, n // ROW_DMA_UNROLL, body, 0)


WEIGHT_CAST_CHUNKS = 4
WEIGHT_DMA_PRIORITY = 1
ROW_SLOTS = 3


class _TilePlan:
    def __init__(self, be_ref, first_ref, ord_ref, next_ref, meta_ref):
        j, i = pl.program_id(0), pl.program_id(1)
        self.j, self.i = j, i
        self.expert = be_ref[i]
        self.first = first_ref[i] == 1
        self.valid = i < meta_ref[0]
        tile = j * meta_ref[1] + ord_ref[i]
        self.slot = tile & 1
        self.is_tile0 = tile == 0
        nxt = next_ref[i]
        more_j = j + 1 < pl.num_programs(0)
        self.has_next = jnp.logical_or(nxt >= 0, more_j)
        self.next_e = jnp.where(nxt >= 0, nxt, be_ref[0])
        self.next_j = jnp.where(nxt >= 0, j, j + 1)


def _first_step_dot(x, stage, w_s):
    k = x.shape[1]
    ck = k // WEIGHT_CAST_CHUNKS
    acc = None
    for c in range(WEIGHT_CAST_CHUNKS):
        wb = stage[pl.ds(c * ck, ck), :].astype(BF16)
        w_s[pl.ds(c * ck, ck), :] = wb
        part = jnp.dot(x[:, c * ck:(c + 1) * ck], wb, preferred_element_type=F32)
        acc = part if acc is None else acc + part
    return acc


def _swiglu(g, u):
    g = jnp.minimum(g, SWIGLU_LIMIT)
    u = jnp.clip(u, -SWIGLU_LIMIT, SWIGLU_LIMIT)
    return (u + 1.0) * (g * jax.nn.sigmoid(SWIGLU_ALPHA * g))


def _gate_up_kernel(be_ref, first_ref, ord_ref, next_ref, meta_ref, *refs, tf, f, j0, gather):
    if gather:
        (tok_ref, tok1_ref, tok2_ref, h_hbm, w_hbm, bg_ref, bu_ref, x_ref, o_ref,
         rows, stage_g, stage_u, wg_s, wu_s, row_sem, sem) = refs
    else:
        x_ref, w_hbm, bg_ref, bu_ref, o_ref, stage_g, stage_u, wg_s, wu_s, sem = refs
    plan = _TilePlan(be_ref, first_ref, ord_ref, next_ref, meta_ref)

    if gather:
        n, half = rows.shape[1], rows.shape[2]
        slot = lax.rem(plan.i, ROW_SLOTS)
        ahead = lax.rem(plan.i + 2, ROW_SLOTS)
        last = pl.num_programs(1) - 1

        @pl.when(plan.i == 0)
        def _():
            _start_rows(h_hbm, rows.at[0], row_sem.at[0], n, lambda r: tok_ref[0, r])
            _start_rows(h_hbm, rows.at[1], row_sem.at[1], n, lambda r: tok1_ref[0, r])

        _wait_rows(h_hbm, rows.at[slot], row_sem.at[slot], n)
        words = rows[slot]
        x_ref[:, :half] = _unpack_half(words, 0).astype(x_ref.dtype)
        x_ref[:, half:] = _unpack_half(words, 1).astype(x_ref.dtype)

    def fetch_next_rows(inline):
        if gather:
            _start_rows(h_hbm, rows.at[ahead], row_sem.at[ahead], n, lambda r: tok2_ref[0, r], inline=inline)

    def tile_copies(e, j, slot):
        col = pl.multiple_of((j + j0) * tf, tf)
        return (pltpu.make_async_copy(w_hbm.at[e, :, pl.ds(col, tf)], stage_g.at[slot], sem.at[slot]),
                pltpu.make_async_copy(w_hbm.at[e, :, pl.ds(f + col, tf)], stage_u.at[slot], sem.at[slot]))

    @pl.when(plan.first)
    def _():
        @pl.when(plan.is_tile0)
        def _():
            for c in tile_copies(plan.expert, plan.j, plan.slot):
                c.start(priority=WEIGHT_DMA_PRIORITY)
        for c in tile_copies(plan.expert, plan.j, plan.slot):
            c.wait()

        @pl.when(plan.has_next)
        def _():
            for c in tile_copies(plan.next_e, plan.next_j, 1 - plan.slot):
                c.start(priority=WEIGHT_DMA_PRIORITY)
        fetch_next_rows(inline=True)
        x = x_ref[...]
        g = _first_step_dot(x, stage_g.at[plan.slot], wg_s) + bg_ref[...]
        u = _first_step_dot(x, stage_u.at[plan.slot], wu_s) + bu_ref[...]
        o_ref[...] = _swiglu(g, u).astype(o_ref.dtype)

    @pl.when(jnp.logical_and(jnp.logical_not(plan.first), plan.valid))
    def _():
        fetch_next_rows(inline=True)
        x = x_ref[...]
        g = jnp.dot(x, wg_s[...], preferred_element_type=F32) + bg_ref[...]
        u = jnp.dot(x, wu_s[...], preferred_element_type=F32) + bu_ref[...]
        o_ref[...] = _swiglu(g, u).astype(o_ref.dtype)

    @pl.when(jnp.logical_not(plan.valid))
    def _():
        fetch_next_rows(inline=False)
        o_ref[...] = jnp.zeros(o_ref.shape, o_ref.dtype)

    if gather:
        @pl.when(plan.i == last)
        def _():
            for s in (lax.rem(plan.i + 1, ROW_SLOTS), ahead):
                _wait_rows(h_hbm, rows.at[s], row_sem.at[s], n)


def _gate_up(x_src, w_gu, b_gu, plan_arrays, cfg, j0, nj, row_tok=None):
    gather = row_tok is not None
    e, f, rb, tf = cfg.n_experts, cfg.d_expert, cfg.moe_rows, cfg.gu_tf
    d = cfg.d_model
    nf = f // tf
    p = row_tok.shape[0] if gather else x_src.shape[0]
    nb = p // rb
    assert d % (WEIGHT_CAST_CHUNKS * 128) == 0 and rb % ROW_DMA_UNROLL == 0
    bias = lambda off: pl.BlockSpec((None, 1, tf), lambda j, i, be, *_: (be[i], 0, off + j0 + j))
    act_spec = pl.BlockSpec((rb, tf), lambda j, i, be, *_: (i, j))
    row_spec = pl.BlockSpec((rb, d), lambda j, i, be, *_: (i, 0))
    weights = [pltpu.VMEM((2, d, tf), F32), pltpu.VMEM((2, d, tf), F32),
               pltpu.VMEM((d, tf), BF16), pltpu.VMEM((d, tf), BF16)]
    act_shape = jax.ShapeDtypeStruct((p, nj * tf), BF16)
    b3 = b_gu.reshape(e, 1, 2 * f)
    if gather:
        assert nj == 1
        tok = row_tok.reshape(nb, 1, rb)
        idx = lambda g: pl.BlockSpec((None, 1, rb), lambda j, i, *_: (g(i), 0, 0), memory_space=pltpu.SMEM)
        assert nb >= ROW_SLOTS
        in_specs = [idx(lambda i: i), idx(lambda i: jnp.minimum(i + 1, nb - 1)),
                    idx(lambda i: jnp.minimum(i + 2, nb - 1)),
                    pl.BlockSpec(memory_space=pl.ANY), pl.BlockSpec(memory_space=pl.ANY), bias(0), bias(nf)]
        out_specs = [row_spec, act_spec]
        out_shape = [jax.ShapeDtypeStruct((p, d), BF16), act_shape]
        scratch = ([pltpu.VMEM((ROW_SLOTS, rb, d // 2), jnp.int32)] + weights
                   + [pltpu.SemaphoreType.DMA((ROW_SLOTS,)), pltpu.SemaphoreType.DMA((2,))])
        operands = (tok, tok, tok, x_src, w_gu, b3, b3)
    else:
        in_specs = [row_spec, pl.BlockSpec(memory_space=pl.ANY), bias(0), bias(nf)]
        out_specs, out_shape = act_spec, act_shape
        scratch = weights + [pltpu.SemaphoreType.DMA((2,))]
        operands = (x_src, w_gu, b3, b3)
    grid_spec = pltpu.PrefetchScalarGridSpec(
        num_scalar_prefetch=5, grid=(nj, nb), in_specs=in_specs, out_specs=out_specs, scratch_shapes=scratch)
    return pl.pallas_call(
        functools.partial(_gate_up_kernel, tf=tf, f=f, j0=j0, gather=gather), grid_spec=grid_spec,
        out_shape=out_shape, compiler_params=_params(("arbitrary", "arbitrary")),
        name="moe_gather_gate_up" if gather else "moe_gate_up",
    )(*plan_arrays, *operands)


def _down_kernel(be_ref, first_ref, ord_ref, next_ref, meta_ref, *refs, tn):
    *a_refs, w_hbm, b_ref, o_ref, stage, w_s, sem = refs
    plan = _TilePlan(be_ref, first_ref, ord_ref, next_ref, meta_ref)

    def act():
        return jnp.concatenate([r[...] for r in a_refs], axis=1) if len(a_refs) > 1 else a_refs[0][...]

    def tile_copy(e, j, slot):
        col = pl.multiple_of(j * tn, tn)
        return pltpu.make_async_copy(w_hbm.at[e, :, pl.ds(col, tn)], stage.at[slot], sem.at[slot])

    @pl.when(plan.first)
    def _():
        @pl.when(plan.is_tile0)
        def _():
            tile_copy(plan.expert, plan.j, plan.slot).start(priority=WEIGHT_DMA_PRIORITY)
        tile_copy(plan.expert, plan.j, plan.slot).wait()

        @pl.when(plan.has_next)
        def _():
            tile_copy(plan.next_e, plan.next_j, 1 - plan.slot).start(priority=WEIGHT_DMA_PRIORITY)
        o_ref[...] = _pack_halves(_first_step_dot(act(), stage.at[plan.slot], w_s) + b_ref[...])

    @pl.when(jnp.logical_and(jnp.logical_not(plan.first), plan.valid))
    def _():
        o_ref[...] = _pack_halves(jnp.dot(act(), w_s[...], preferred_element_type=F32) + b_ref[...])

    @pl.when(jnp.logical_not(plan.valid))
    def _():
        o_ref[...] = jnp.zeros(o_ref.shape, o_ref.dtype)


def _down(acts, w_dn, b_dn, plan_arrays, cfg):
    p = acts[0].shape[0]
    f = sum(a.shape[1] for a in acts)
    e, d, rb, tn = cfg.n_experts, cfg.d_model, cfg.moe_rows, cfg.dn_tn
    assert f == cfg.d_expert and f % (WEIGHT_CAST_CHUNKS * 128) == 0
    grid_spec = pltpu.PrefetchScalarGridSpec(
        num_scalar_prefetch=5, grid=(d // tn, p // rb),
        in_specs=[pl.BlockSpec((rb, a.shape[1]), lambda j, i, be, *_: (i, 0)) for a in acts] + [
                  pl.BlockSpec(memory_space=pl.ANY),
                  pl.BlockSpec((None, 1, tn), lambda j, i, be, *_: (be[i], 0, j))],
        out_specs=pl.BlockSpec((rb, tn // 2), lambda j, i, be, *_: (i, j)),
        scratch_shapes=[pltpu.VMEM((2, f, tn), F32), pltpu.VMEM((f, tn), BF16),
                        pltpu.SemaphoreType.DMA((2,))])
    return pl.pallas_call(
        functools.partial(_down_kernel, tn=tn), grid_spec=grid_spec,
        out_shape=jax.ShapeDtypeStruct((p, d // 2), jnp.int32),
        compiler_params=_params(("arbitrary", "arbitrary")), name="moe_down",
    )(*plan_arrays, *acts, w_dn, b_dn.reshape(e, 1, d))


def _combine_kernel(pos_ref, pos_next_ref, ys_hbm, h_ref, gate_ref, g2_ref, b2_ref, o_ref, buf, sem, *,
                    alpha, tile):
    i = pl.program_id(0)
    top_k, n = buf.shape[1], buf.shape[2]
    slot = i % 2
    hw = tile // 2

    def start(idx_ref, s, inline):
        for k in range(top_k):
            _start_rows(ys_hbm, buf.at[s, k], sem.at[s], n, lambda t, k=k: idx_ref[k, t], inline=inline)

    def wait(s):
        for k in range(top_k):
            _wait_rows(ys_hbm, buf.at[s, k], sem.at[s], n)

    @pl.when(i == 0)
    def _():
        start(pos_ref, 0, inline=False)

    wait(slot)
    start(pos_next_ref, 1 - slot, inline=True)
    gates = gate_ref[...]
    slabs = []
    for j in range(buf.shape[3] // hw):
        words = [buf[slot, k, :, j * hw:(j + 1) * hw] for k in range(top_k)]
        for half in (0, 1):
            terms = [gates[:, k:k + 1] * _unpack_half(words[k], half) for k in range(top_k)]
            slabs.append(functools.reduce(lambda a, b: a + b, terms))
    y = jnp.concatenate(slabs, axis=1)
    o_ref[...] = _layer_norm_rows(alpha * h_ref[...] + y, g2_ref[...], b2_ref[...])

    @pl.when(i == pl.num_programs(0) - 1)
    def _():
        wait(1 - slot)


def _combine(ys, pos, gates_tk, h1, g2, b2, cfg):
    t, d = h1.shape
    k, n = cfg.top_k, cfg.combine_rows
    steps = t // n
    assert n % ROW_DMA_UNROLL == 0
    alpha = (2 * cfg.depth) ** 0.25
    pos_blocks = pos.reshape(k, steps, n).transpose(1, 0, 2)
    idx = lambda f: pl.BlockSpec((None, k, n), lambda i: (f(i), 0, 0), memory_space=pltpu.SMEM)
    row = pl.BlockSpec((n, d), lambda i: (i, 0))
    vec = pl.BlockSpec((1, d), lambda i: (0, 0))
    return pl.pallas_call(
        functools.partial(_combine_kernel, alpha=alpha, tile=cfg.dn_tn), grid=(steps,),
        in_specs=[idx(lambda i: i), idx(lambda i: jnp.minimum(i + 1, steps - 1)),
                  pl.BlockSpec(memory_space=pl.ANY), row,
                  pl.BlockSpec((n, k), lambda i: (i, 0)), vec, vec],
        out_specs=row,
        out_shape=jax.ShapeDtypeStruct((t, d), F32),
        scratch_shapes=[pltpu.VMEM((2, k, n, d // 2), jnp.int32), pltpu.SemaphoreType.DMA((2,))],
        compiler_params=_params(("arbitrary",)), name="moe_combine_ln2",
    )(pos_blocks, pos_blocks, ys, h1, gates_tk, g2.reshape(1, d), b2.reshape(1, d))


def _forward(cfg, x, ln_in_g, ln_in_b, w_in, conv_w, w_out, ln1_g, ln1_b, router_w, router_b,
             w_gate_up, b_gate_up, w_down, b_down, ln2_g, ln2_b):
    assert cfg.depth == 1
    b, s, d = x.shape
    x2d = x.reshape(b * s, d)
    slopes = jnp.asarray(_alibi_slopes(cfg.n_heads))
    h0 = _ln_in(x2d, ln_in_g, ln_in_b, cfg)
    proj = _matmul([h0], w_in[0], cfg, F32, "in_proj")
    attn = _attention(proj, slopes, cfg)
    y_conv = _gated_conv(proj, conv_w[0], cfg)
    mix = _matmul([attn, y_conv], w_out[0], cfg, F32, "out_proj")
    h1, h1_packed, top_idx, gates = _ln1_router(x2d, mix, ln_in_g, ln_in_b, ln1_g[0], ln1_b[0],
                                                router_w[0], router_b[0], cfg)
    row_tok, pos, *plan_arrays = _routing_plan(top_idx, cfg)
    n_tiles = cfg.d_expert // cfg.gu_tf
    xs, act0 = _gate_up(h1_packed, w_gate_up[0], b_gate_up[0], plan_arrays, cfg, 0, 1, row_tok=row_tok)
    acts = [act0]
    if n_tiles > 1:
        acts.append(_gate_up(xs, w_gate_up[0], b_gate_up[0], plan_arrays, cfg, 1, n_tiles - 1))
    ys = _down(acts, w_down[0], b_down[0], plan_arrays, cfg)
    out = _combine(ys, pos, gates.T, h1, ln2_g[0], ln2_b[0], cfg)
    return out.reshape(b, s, d)


_CONFIG = Config(
    batch=4, seq=2048, d_model=4096, head_dim=128, n_heads=24, conv_width=1024,
    patterns=((128, 1), (512, 4), (2048, 16)), n_experts=32, top_k=4, d_expert=2048, depth=1,
    ln_rows=256, mm_tm=1024, mm_tn=512, conv_cw=256, moe_rows=256, gu_tf=512, dn_tn=2048,
    gather_rows=256, combine_rows=64)


def kernel(x, ln_in_g, ln_in_b, w_in, conv_w, w_out, ln1_g, ln1_b, router_w, router_b,
           w_gate_up, b_gate_up, w_down, b_down, ln2_g, ln2_b):
    return _forward(_CONFIG, x, ln_in_g, ln_in_b, w_in, conv_w, w_out, ln1_g, ln1_b, router_w, router_b,
                    w_gate_up, b_gate_up, w_down, b_down, ln2_g, ln2_b)
```

```python
import dataclasses
import functools
import math

import numpy as np
import jax
import jax.numpy as jnp
from jax import lax
from jax.experimental import pallas as pl
from jax.experimental.pallas import tpu as pltpu

F32 = jnp.float32
BF16 = jnp.bfloat16

LN_EPS = 1e-5
SWIGLU_LIMIT = 7.0
SWIGLU_ALPHA = 1.702
MASKED_SCORE = -1e30
V7X_VMEM_LIMIT_BYTES = 56 * 1024 * 1024


@dataclasses.dataclass(frozen=True)
class Config:
    batch: int
    seq: int
    d_model: int
    head_dim: int
    n_heads: int
    conv_width: int
    patterns: tuple
    n_experts: int
    top_k: int
    d_expert: int
    depth: int
    ln_rows: int
    mm_tm: int
    mm_tn: int
    conv_cw: int
    moe_rows: int
    gu_tf: int
    dn_tn: int
    gather_rows: int
    combine_rows: int

    @property
    def attn_width(self):
        return self.n_heads * self.head_dim

    @property
    def tokens(self):
        return self.batch * self.seq


def _params(sem, vmem=V7X_VMEM_LIMIT_BYTES):
    return pltpu.CompilerParams(dimension_semantics=sem, vmem_limit_bytes=vmem)


def _alibi_slopes(n):
    def pow2(m):
        start = 2.0 ** (-(2.0 ** -(math.log2(m) - 3)))
        return [start ** (i + 1) for i in range(m)]
    if math.log2(n).is_integer():
        s = pow2(n)
    else:
        c = 2 ** int(math.floor(math.log2(n)))
        s = pow2(c) + pow2(2 * c)[0::2][: n - c]
    return np.asarray(s, dtype=np.float32)


def _pack_halves(x):
    n = x.shape[1] // 2
    return pltpu.pack_elementwise([x[:, :n], x[:, n:]], packed_dtype=BF16)


def _unpack_half(words, half):
    return pltpu.unpack_elementwise(words, index=half, packed_dtype=BF16, unpacked_dtype=F32)


def _layer_norm_rows(x, g, b):
    mu = jnp.mean(x, axis=-1, keepdims=True)
    xc = x - mu
    var = jnp.mean(xc * xc, axis=-1, keepdims=True)
    return xc * lax.rsqrt(var + LN_EPS) * g + b


def _ln_in_kernel(x_ref, g_ref, b_ref, o_ref):
    o_ref[...] = _layer_norm_rows(x_ref[...], g_ref[...], b_ref[...]).astype(o_ref.dtype)


def _ln_in(x2d, g, b, cfg):
    t, d = x2d.shape
    tm = cfg.ln_rows
    row = pl.BlockSpec((tm, d), lambda i: (i, 0))
    vec = pl.BlockSpec((1, d), lambda i: (0, 0))
    return pl.pallas_call(
        _ln_in_kernel, grid=(t // tm,), in_specs=[row, vec, vec], out_specs=row,
        out_shape=jax.ShapeDtypeStruct((t, d), BF16),
        compiler_params=_params(("parallel",)), name="ln_in",
    )(x2d, g.reshape(1, d), b.reshape(1, d))


def _matmul_kernel(*refs):
    *xw_refs, o_ref = refs
    n_in = len(xw_refs) // 2
    acc = None
    for x_ref, w_ref in zip(xw_refs[:n_in], xw_refs[n_in:]):
        part = jnp.dot(x_ref[...], w_ref[...].astype(BF16), preferred_element_type=F32)
        acc = part if acc is None else acc + part
    o_ref[...] = acc.astype(o_ref.dtype)


def _matmul(xs, w, cfg, out_dtype, name):
    m = xs[0].shape[0]
    n = w.shape[1]
    tm, tn = cfg.mm_tm, cfg.mm_tn
    x_specs, w_specs, row0 = [], [], 0
    for x in xs:
        k = x.shape[1]
        assert row0 % k == 0
        x_specs.append(pl.BlockSpec((tm, k), lambda i, j: (i, 0)))
        w_specs.append(pl.BlockSpec((k, tn), lambda i, j, r=row0 // k: (r, j)))
        row0 += k
    assert row0 == w.shape[0]
    return pl.pallas_call(
        _matmul_kernel, grid=(m // tm, n // tn),
        in_specs=x_specs + w_specs,
        out_specs=pl.BlockSpec((tm, tn), lambda i, j: (i, j)),
        out_shape=jax.ShapeDtypeStruct((m, n), out_dtype),
        compiler_params=_params(("parallel", "parallel")), name=name,
    )(*xs, *([w] * len(xs)))


def _attn_kernel(slopes_ref, q_ref, k_ref, v_ref, o_ref, acc_ref, m_ref, l_ref, *, cfg, blk):
    seq, hd = cfg.seq, cfg.head_dim
    nblk = seq // blk
    slope = slopes_ref[pl.program_id(1)]
    scale = 1.0 / math.sqrt(hd)

    qi = lax.broadcasted_iota(jnp.int32, (blk, blk), 0)
    ki = lax.broadcasted_iota(jnp.int32, (blk, blk), 1)
    d_own = qi - ki
    own_steps = d_own.astype(F32)
    prev_steps = (d_own + blk).astype(F32)

    for g, (window, dil) in enumerate(cfg.patterns):
        sub = seq // dil
        nb = sub // blk
        neg = -slope * float(dil)

        def to_blocks(ref, f, dil=dil, sub=sub, nb=nb):
            if dil == 1:
                return f(ref[...]).reshape(nblk, blk, hd)
            return jnp.concatenate(
                [f(ref[pl.ds(r, sub, stride=dil), :]).reshape(nb, blk, hd) for r in range(dil)], axis=0)

        def from_blocks(dst, x, g=g, dil=dil, sub=sub, nb=nb):
            if dil == 1:
                dst[g] = x.reshape(seq, hd)
            else:
                for r in range(dil):
                    dst[g, pl.ds(r, sub, stride=dil), :] = x[r * nb:(r + 1) * nb].reshape(sub, hd)

        q = to_blocks(q_ref, lambda x: (x * scale).astype(BF16))
        keys = to_blocks(k_ref, lambda x: x.astype(BF16))
        vals = to_blocks(v_ref, lambda x: x.astype(BF16))
        bias = jnp.where(d_own >= 0, own_steps * neg, MASKED_SCORE)[None]
        if nb > 1:
            def with_prev(x, dil=dil, nb=nb):
                x4 = x.reshape(dil, nb, blk, hd)
                prev = jnp.concatenate([jnp.zeros((dil, 1, blk, hd), x.dtype), x4[:, :-1]], axis=1)
                return jnp.concatenate([prev.reshape(nblk, blk, hd), x], axis=1)
            keys, vals = with_prev(keys), with_prev(vals)
            n_of_block = lax.broadcasted_iota(jnp.int32, (nblk, blk, blk), 0) & (nb - 1)
            prev_ok = jnp.logical_and(d_own[None] <= 0, n_of_block > 0)
            bias_prev = jnp.where(prev_ok, (prev_steps * neg)[None], MASKED_SCORE)
            bias = jnp.concatenate([bias_prev, jnp.broadcast_to(bias, (nblk, blk, blk))], axis=2)
        s = jnp.einsum("bqd,bkd->bqk", q, keys, preferred_element_type=F32) + bias
        m = jnp.max(s, axis=-1, keepdims=True)
        p = jnp.exp(s - m).astype(BF16)
        vals1 = jnp.concatenate([vals, jnp.ones(vals.shape, BF16)], axis=-1)
        pv = jnp.einsum("bqk,bkd->bqd", p, vals1, preferred_element_type=F32)
        from_blocks(acc_ref, pv[:, :, :hd])
        from_blocks(l_ref, pv[:, :, hd:])
        from_blocks(m_ref, jnp.broadcast_to(m, (nblk, blk, hd)))

    n_win = len(cfg.patterns)
    gs = max(c for c in range(1, 5) if nblk % c == 0)

    def merge(i, carry):
        for j in range(gs):
            r = pl.ds(pl.multiple_of((i * gs + j) * blk, blk), blk)
            ms = [m_ref[g, r, :] for g in range(n_win)]
            m_all = functools.reduce(jnp.maximum, ms)
            ws = [jnp.exp(m - m_all) for m in ms]
            num = functools.reduce(lambda a, b: a + b, [w * acc_ref[g, r, :] for g, w in enumerate(ws)])
            den = functools.reduce(lambda a, b: a + b, [w * l_ref[g, r, :] for g, w in enumerate(ws)])
            o_ref[r, :] = (num / den).astype(o_ref.dtype)
        return carry
    lax.fori_loop(0, nblk // gs, merge, 0)


def _attention(proj, slopes, cfg):
    seq, hd, nh = cfg.seq, cfg.head_dim, cfg.n_heads
    blks = {w // d for w, d in cfg.patterns}
    assert len(blks) == 1, "all windows must share one block length"
    blk = blks.pop()
    for w, d in cfg.patterns:
        nb = seq // (d * blk)
        assert d & (d - 1) == 0 and nb & (nb - 1) == 0 and nb * d * blk == seq
    grid_spec = pltpu.PrefetchScalarGridSpec(
        num_scalar_prefetch=1, grid=(cfg.batch, nh),
        in_specs=[pl.BlockSpec((seq, hd), lambda b, h, s: (b, h)),
                  pl.BlockSpec((seq, hd), lambda b, h, s: (b, nh + h)),
                  pl.BlockSpec((seq, hd), lambda b, h, s: (b, 2 * nh + h))],
        out_specs=pl.BlockSpec((seq, hd), lambda b, h, s: (b, h)),
        scratch_shapes=[pltpu.VMEM((len(cfg.patterns), seq, hd), F32)] * 3)
    return pl.pallas_call(
        functools.partial(_attn_kernel, cfg=cfg, blk=blk), grid_spec=grid_spec,
        out_shape=jax.ShapeDtypeStruct((cfg.tokens, cfg.attn_width), BF16),
        compiler_params=_params(("parallel", "parallel")), name="dilated_attention",
    )(slopes, proj, proj, proj)


def _conv_kernel(gb_ref, gc_ref, val_ref, w_ref, o_ref):
    u = gc_ref[...] * val_ref[...]
    row = lax.broadcasted_iota(jnp.int32, u.shape, 0)
    u1 = jnp.where(row >= 1, pltpu.roll(u, 1, 0), 0.0)
    u2 = jnp.where(row >= 2, pltpu.roll(u, 2, 0), 0.0)
    w = w_ref[...]
    z = w[0:1, :] * u2 + w[1:2, :] * u1 + w[2:3, :] * u
    o_ref[...] = (gb_ref[...] * z).astype(o_ref.dtype)


def _gated_conv(proj, conv_w, cfg):
    seq, cw, c = cfg.seq, cfg.conv_cw, cfg.conv_width
    base = 3 * cfg.attn_width // cw
    per = c // cw
    spec = lambda off: pl.BlockSpec((seq, cw), lambda b, j, off=off: (b, base + off * per + j))
    return pl.pallas_call(
        _conv_kernel, grid=(cfg.batch, per),
        in_specs=[spec(0), spec(1), spec(2), pl.BlockSpec((conv_w.shape[0], cw), lambda b, j: (0, j))],
        out_specs=pl.BlockSpec((seq, cw), lambda b, j: (b, j)),
        out_shape=jax.ShapeDtypeStruct((cfg.tokens, c), BF16),
        compiler_params=_params(("parallel", "parallel")), name="gated_conv",
    )(proj, proj, proj, conv_w)


def _ln1_router_kernel(x_ref, mix_ref, gi_ref, bi_ref, g1_ref, b1_ref, rw_ref, rb_ref,
                       h_ref, hp_ref, idx_ref, gate_ref, *, alpha, top_k):
    h0 = _layer_norm_rows(x_ref[...], gi_ref[...], bi_ref[...])
    h1 = _layer_norm_rows(alpha * h0 + mix_ref[...], g1_ref[...], b1_ref[...])
    h_ref[...] = h1
    hp_ref[...] = _pack_halves(h1)
    def split(v):
        hi = v.astype(BF16)
        return hi, (v - hi.astype(F32)).astype(BF16)

    def dot_t(a, b):
        return lax.dot_general(a, b, (((1,), (1,)), ((), ())), preferred_element_type=F32)
    w_hi, w_lo = split(rw_ref[...])
    h_hi, h_lo = split(h1)
    logits = dot_t(w_hi, h_hi) + (dot_t(w_hi, h_lo) + dot_t(w_lo, h_hi)) + rb_ref[...]
    n_e = logits.shape[0]
    eidx = lax.broadcasted_iota(jnp.int32, logits.shape, 0)
    vals, idxs = [], []
    for _ in range(top_k):
        m = jnp.max(logits, axis=0, keepdims=True)
        i = jnp.min(jnp.where(logits == m, eidx, n_e), axis=0, keepdims=True)
        vals.append(m)
        idxs.append(i)
        logits = jnp.where(eidx == i, -jnp.inf, logits)
    ex = [jnp.exp(v - vals[0]) for v in vals]
    den = functools.reduce(lambda a, b: a + b, ex)
    idx_ref[...] = jnp.concatenate(idxs, axis=0)
    gate_ref[...] = jnp.concatenate([e / den for e in ex], axis=0)


def _ln1_router(x2d, mix, gi, bi, g1, b1, router_w, router_b, cfg):
    t, d = x2d.shape
    e, k = cfg.n_experts, cfg.top_k
    tm = cfg.ln_rows
    alpha = (2 * cfg.depth) ** 0.25
    row = pl.BlockSpec((tm, d), lambda i: (i, 0))
    vec = pl.BlockSpec((1, d), lambda i: (0, 0))
    sel = pl.BlockSpec((k, tm), lambda i: (0, i))
    return pl.pallas_call(
        functools.partial(_ln1_router_kernel, alpha=alpha, top_k=k), grid=(t // tm,),
        in_specs=[row, row, vec, vec, vec, vec,
                  pl.BlockSpec((e, d), lambda i: (0, 0)), pl.BlockSpec((e, 1), lambda i: (0, 0))],
        out_specs=[row, pl.BlockSpec((tm, d // 2), lambda i: (i, 0)), sel, sel],
        out_shape=[jax.ShapeDtypeStruct((t, d), F32), jax.ShapeDtypeStruct((t, d // 2), jnp.int32),
                   jax.ShapeDtypeStruct((k, t), jnp.int32),
                   jax.ShapeDtypeStruct((k, t), F32)],
        compiler_params=_params(("parallel",)), name="ln1_router",
    )(x2d, mix, gi.reshape(1, d), bi.reshape(1, d), g1.reshape(1, d), b1.reshape(1, d),
      router_w.T, router_b.reshape(e, 1))


def _routing_plan(top_idx, cfg):
    k, t = top_idx.shape
    e, rb = cfg.n_experts, cfg.moe_rows
    a = k * t
    nb = a // rb + e
    p = nb * rb
    i32 = jnp.int32
    flat_e = top_idx.reshape(a)
    slot = jnp.arange(a, dtype=i32)
    experts = jnp.arange(e, dtype=i32)
    _, order = lax.sort((flat_e, slot), num_keys=1)
    counts = jnp.sum((flat_e[:, None] == experts[None, :]).astype(i32), axis=0)
    ends = jnp.cumsum(counts)
    padded = (counts + rb - 1) // rb * rb
    pends = jnp.cumsum(padded)
    pstarts = pends - padded
    gap = padded - counts

    def gaps_before(x, bounds):
        return jnp.sum(jnp.where(x[:, None] >= bounds[None, :], gap[None, :], 0), axis=1)
    pos_sorted = slot + gaps_before(slot, ends)
    _, pos = lax.sort((order, pos_sorted), num_keys=1)

    rows = jnp.arange(p, dtype=i32)
    src = rows - gaps_before(rows, pends)
    seg_end = jnp.sum(jnp.where(rows[:, None] >= pstarts[None, :], counts[None, :], 0), axis=1)
    row_tok = jnp.where(src < seg_end, order[jnp.clip(src, 0, a - 1)] % t, 0).astype(i32)

    n_blocks = pends[-1] // rb
    bi = jnp.arange(nb, dtype=i32)
    raw_e = jnp.minimum(jnp.sum((bi[:, None] * rb >= pends[None, :]).astype(i32), axis=1), e - 1)
    block_e = jnp.where(bi < n_blocks, raw_e, raw_e[n_blocks - 1])
    first = jnp.concatenate([jnp.ones((1,), i32), (block_e[1:] != block_e[:-1]).astype(i32)])
    tile_ord = jnp.cumsum(first) - 1
    tile_e = jnp.max(jnp.where(tile_ord[None, :] == bi[:, None], block_e[None, :], -1), axis=1)
    next_e = tile_e[jnp.minimum(tile_ord + 1, nb - 1)]
    meta = jnp.stack([n_blocks, tile_ord[-1] + 1]).astype(i32)
    return (row_tok, pos.reshape(k, t).astype(i32), block_e.astype(i32), first, tile_ord.astype(i32),
            next_e.astype(i32), meta)


ROW_DMA_UNROLL = 8


def _row_copy(src_hbm, dst_vmem, sem, src_row, dst_row):
    return pltpu.make_async_copy(src_hbm.at[pl.ds(src_row, 1), :], dst_vmem.at[pl.ds(dst_row, 1), :], sem)


def _start_rows(src_hbm, dst_vmem, sem, n, src_row_of, inline=False, queues=(0, 1)):
    if inline:
        for r in range(n):
            _row_copy(src_hbm, dst_vmem, sem, src_row_of(r), r).start(priority=queues[r % len(queues)])
        return

    def body(c, carry):
        for u in range(ROW_DMA_UNROLL):
            r = c * ROW_DMA_UNROLL + u
            _row_copy(src_hbm, dst_vmem, sem, src_row_of(r), r).start(priority=queues[u % len(queues)])
        return carry
    lax.fori_loop(0, n // ROW_DMA_UNROLL, body, 0)


def _wait_rows(src_hbm, dst_vmem, sem, n):
    def body(c, carry):
        for u in range(ROW_DMA_UNROLL):
            _row_copy(src_hbm, dst_vmem, sem, 0, c * ROW_DMA_UNROLL + u).wait()
        return carry
    lax.fori_loop(0, n // ROW_DMA_UNROLL, body, 0)


WEIGHT_CAST_CHUNKS = 4
WEIGHT_DMA_PRIORITY = 1

class _TilePlan:
    def __init__(self, be_ref, first_ref, ord_ref, next_ref, meta_ref):
        j, i = pl.program_id(0), pl.program_id(1)
        self.j, self.i = j, i
        self.expert = be_ref[i]
        self.first = first_ref[i] == 1
        self.valid = i < meta_ref[0]
        tile = j * meta_ref[1] + ord_ref[i]
        self.slot = tile & 1
        self.is_tile0 = tile == 0
        nxt = next_ref[i]
        more_j = j + 1 < pl.num_programs(0)
        self.has_next = jnp.logical_or(nxt >= 0, more_j)
        self.next_e = jnp.where(nxt >= 0, nxt, be_ref[0])
        self.next_j = jnp.where(nxt >= 0, j, j + 1)


def _first_step_dot(x, stage, w_s):
    k = x.shape[1]
    ck = k // WEIGHT_CAST_CHUNKS
    acc = None
    for c in range(WEIGHT_CAST_CHUNKS):
        wb = stage[pl.ds(c * ck, ck), :].astype(BF16)
        w_s[pl.ds(c * ck, ck), :] = wb
        part = jnp.dot(x[:, c * ck:(c + 1) * ck], wb, preferred_element_type=F32)
        acc = part if acc is None else acc + part
    return acc


def _swiglu(g, u):
    g = jnp.minimum(g, SWIGLU_LIMIT)
    u = jnp.clip(u, -SWIGLU_LIMIT, SWIGLU_LIMIT)
    return (u + 1.0) * (g * jax.nn.sigmoid(SWIGLU_ALPHA * g))


def _gate_up_kernel(be_ref, first_ref, ord_ref, next_ref, meta_ref, *refs, tf, f, j0, gather):
    if gather:
        (tok_ref, tok_next_ref, h_hbm, w_hbm, bg_ref, bu_ref, x_ref, o_ref,
         rows, stage_g, stage_u, wg_s, wu_s, row_sem, sem) = refs
    else:
        x_ref, w_hbm, bg_ref, bu_ref, o_ref, stage_g, stage_u, wg_s, wu_s, sem = refs
    plan = _TilePlan(be_ref, first_ref, ord_ref, next_ref, meta_ref)

    if gather:
        n, half = rows.shape[1], rows.shape[2]
        slot = plan.i % 2
        row_queue = (1 - WEIGHT_DMA_PRIORITY,)

        @pl.when(plan.i == 0)
        def _():
            _start_rows(h_hbm, rows.at[0], row_sem.at[0], n, lambda r: tok_ref[0, r], queues=row_queue)

        @pl.when(plan.i + 1 < meta_ref[0])
        def _():
            _start_rows(h_hbm, rows.at[1 - slot], row_sem.at[1 - slot], n, lambda r: tok_next_ref[0, r],
                        queues=row_queue)

        @pl.when(plan.valid)
        def _():
            _wait_rows(h_hbm, rows.at[slot], row_sem.at[slot], n)
            words = rows[slot]
            x_ref[:, :half] = _unpack_half(words, 0).astype(x_ref.dtype)
            x_ref[:, half:] = _unpack_half(words, 1).astype(x_ref.dtype)

        @pl.when(jnp.logical_not(plan.valid))
        def _():
            x_ref[...] = jnp.zeros(x_ref.shape, x_ref.dtype)

    def tile_copies(e, j, slot):
        col = pl.multiple_of((j + j0) * tf, tf)
        return (pltpu.make_async_copy(w_hbm.at[e, :, pl.ds(col, tf)], stage_g.at[slot], sem.at[slot]),
                pltpu.make_async_copy(w_hbm.at[e, :, pl.ds(f + col, tf)], stage_u.at[slot], sem.at[slot]))

    @pl.when(plan.first)
    def _():
        @pl.when(plan.is_tile0)
        def _():
            for c in tile_copies(plan.expert, plan.j, plan.slot):
                c.start(priority=WEIGHT_DMA_PRIORITY)
        for c in tile_copies(plan.expert, plan.j, plan.slot):
            c.wait()

        @pl.when(plan.has_next)
        def _():
            for c in tile_copies(plan.next_e, plan.next_j, 1 - plan.slot):
                c.start(priority=WEIGHT_DMA_PRIORITY)
        x = x_ref[...]
        g = _first_step_dot(x, stage_g.at[plan.slot], wg_s) + bg_ref[...]
        u = _first_step_dot(x, stage_u.at[plan.slot], wu_s) + bu_ref[...]
        o_ref[...] = _swiglu(g, u).astype(o_ref.dtype)

    @pl.when(jnp.logical_and(jnp.logical_not(plan.first), plan.valid))
    def _():
        x = x_ref[...]
        g = jnp.dot(x, wg_s[...], preferred_element_type=F32) + bg_ref[...]
        u = jnp.dot(x, wu_s[...], preferred_element_type=F32) + bu_ref[...]
        o_ref[...] = _swiglu(g, u).astype(o_ref.dtype)

    @pl.when(jnp.logical_not(plan.valid))
    def _():
        o_ref[...] = jnp.zeros(o_ref.shape, o_ref.dtype)


def _gate_up(x_src, w_gu, b_gu, plan_arrays, cfg, j0, nj, row_tok=None):
    gather = row_tok is not None
    e, f, rb, tf = cfg.n_experts, cfg.d_expert, cfg.moe_rows, cfg.gu_tf
    d = cfg.d_model
    nf = f // tf
    p = row_tok.shape[0] if gather else x_src.shape[0]
    nb = p // rb
    assert d % (WEIGHT_CAST_CHUNKS * 128) == 0 and rb % ROW_DMA_UNROLL == 0
    bias = lambda off: pl.BlockSpec((None, 1, tf), lambda j, i, be, *_: (be[i], 0, off + j0 + j))
    act_spec = pl.BlockSpec((rb, tf), lambda j, i, be, *_: (i, j))
    row_spec = pl.BlockSpec((rb, d), lambda j, i, be, *_: (i, 0))
    weights = [pltpu.VMEM((2, d, tf), F32), pltpu.VMEM((2, d, tf), F32),
               pltpu.VMEM((d, tf), BF16), pltpu.VMEM((d, tf), BF16)]
    act_shape = jax.ShapeDtypeStruct((p, nj * tf), BF16)
    b3 = b_gu.reshape(e, 1, 2 * f)
    if gather:
        assert nj == 1
        tok = row_tok.reshape(nb, 1, rb)
        idx = lambda g: pl.BlockSpec((None, 1, rb), lambda j, i, *_: (g(i), 0, 0), memory_space=pltpu.SMEM)
        in_specs = [idx(lambda i: i), idx(lambda i: jnp.minimum(i + 1, nb - 1)),
                    pl.BlockSpec(memory_space=pl.ANY), pl.BlockSpec(memory_space=pl.ANY), bias(0), bias(nf)]
        out_specs = [row_spec, act_spec]
        out_shape = [jax.ShapeDtypeStruct((p, d), BF16), act_shape]
        scratch = [pltpu.VMEM((2, rb, d // 2), jnp.int32)] + weights + [pltpu.SemaphoreType.DMA((2,))] * 2
        operands = (tok, tok, x_src, w_gu, b3, b3)
    else:
        in_specs = [row_spec, pl.BlockSpec(memory_space=pl.ANY), bias(0), bias(nf)]
        out_specs, out_shape = act_spec, act_shape
        scratch = weights + [pltpu.SemaphoreType.DMA((2,))]
        operands = (x_src, w_gu, b3, b3)
    grid_spec = pltpu.PrefetchScalarGridSpec(
        num_scalar_prefetch=5, grid=(nj, nb), in_specs=in_specs, out_specs=out_specs, scratch_shapes=scratch)
    return pl.pallas_call(
        functools.partial(_gate_up_kernel, tf=tf, f=f, j0=j0, gather=gather), grid_spec=grid_spec,
        out_shape=out_shape, compiler_params=_params(("arbitrary", "arbitrary")),
        name="moe_gather_gate_up" if gather else "moe_gate_up",
    )(*plan_arrays, *operands)


def _down_kernel(be_ref, first_ref, ord_ref, next_ref, meta_ref, *refs, tn):
    *a_refs, w_hbm, b_ref, o_ref, stage, w_s, sem = refs
    plan = _TilePlan(be_ref, first_ref, ord_ref, next_ref, meta_ref)

    def act():
        return jnp.concatenate([r[...] for r in a_refs], axis=1) if len(a_refs) > 1 else a_refs[0][...]

    def tile_copy(e, j, slot):
        col = pl.multiple_of(j * tn, tn)
        return pltpu.make_async_copy(w_hbm.at[e, :, pl.ds(col, tn)], stage.at[slot], sem.at[slot])

    @pl.when(plan.first)
    def _():
        @pl.when(plan.is_tile0)
        def _():
            tile_copy(plan.expert, plan.j, plan.slot).start(priority=WEIGHT_DMA_PRIORITY)
        tile_copy(plan.expert, plan.j, plan.slot).wait()

        @pl.when(plan.has_next)
        def _():
            tile_copy(plan.next_e, plan.next_j, 1 - plan.slot).start(priority=WEIGHT_DMA_PRIORITY)
        o_ref[...] = _pack_halves(_first_step_dot(act(), stage.at[plan.slot], w_s) + b_ref[...])

    @pl.when(jnp.logical_and(jnp.logical_not(plan.first), plan.valid))
    def _():
        o_ref[...] = _pack_halves(jnp.dot(act(), w_s[...], preferred_element_type=F32) + b_ref[...])

    @pl.when(jnp.logical_not(plan.valid))
    def _():
        o_ref[...] = jnp.zeros(o_ref.shape, o_ref.dtype)


def _down(acts, w_dn, b_dn, plan_arrays, cfg):
    p = acts[0].shape[0]
    f = sum(a.shape[1] for a in acts)
    e, d, rb, tn = cfg.n_experts, cfg.d_model, cfg.moe_rows, cfg.dn_tn
    assert f == cfg.d_expert and f % (WEIGHT_CAST_CHUNKS * 128) == 0
    grid_spec = pltpu.PrefetchScalarGridSpec(
        num_scalar_prefetch=5, grid=(d // tn, p // rb),
        in_specs=[pl.BlockSpec((rb, a.shape[1]), lambda j, i, be, *_: (i, 0)) for a in acts] + [
                  pl.BlockSpec(memory_space=pl.ANY),
                  pl.BlockSpec((None, 1, tn), lambda j, i, be, *_: (be[i], 0, j))],
        out_specs=pl.BlockSpec((rb, tn // 2), lambda j, i, be, *_: (i, j)),
        scratch_shapes=[pltpu.VMEM((2, f, tn), F32), pltpu.VMEM((f, tn), BF16),
                        pltpu.SemaphoreType.DMA((2,))])
    return pl.pallas_call(
        functools.partial(_down_kernel, tn=tn), grid_spec=grid_spec,
        out_shape=jax.ShapeDtypeStruct((p, d // 2), jnp.int32),
        compiler_params=_params(("arbitrary", "arbitrary")), name="moe_down",
    )(*plan_arrays, *acts, w_dn, b_dn.reshape(e, 1, d))


def _combine_kernel(pos_ref, pos_next_ref, ys_hbm, h_ref, gate_ref, g2_ref, b2_ref, o_ref, buf, sem, *,
                    alpha, tile):
    i = pl.program_id(0)
    top_k, n = buf.shape[1], buf.shape[2]
    slot = i % 2
    hw = tile // 2

    def start(idx_ref, s, inline):
        for k in range(top_k):
            _start_rows(ys_hbm, buf.at[s, k], sem.at[s], n, lambda t, k=k: idx_ref[k, t], inline=inline)

    def wait(s):
        for k in range(top_k):
            _wait_rows(ys_hbm, buf.at[s, k], sem.at[s], n)

    @pl.when(i == 0)
    def _():
        start(pos_ref, 0, inline=False)

    wait(slot)
    start(pos_next_ref, 1 - slot, inline=True)
    gates = gate_ref[...]
    slabs = []
    for j in range(buf.shape[3] // hw):
        words = [buf[slot, k, :, j * hw:(j + 1) * hw] for k in range(top_k)]
        for half in (0, 1):
            terms = [gates[:, k:k + 1] * _unpack_half(words[k], half) for k in range(top_k)]
            slabs.append(functools.reduce(lambda a, b: a + b, terms))
    y = jnp.concatenate(slabs, axis=1)
    o_ref[...] = _layer_norm_rows(alpha * h_ref[...] + y, g2_ref[...], b2_ref[...])

    @pl.when(i == pl.num_programs(0) - 1)
    def _():
        wait(1 - slot)


def _combine(ys, pos, gates_tk, h1, g2, b2, cfg):
    t, d = h1.shape
    k, n = cfg.top_k, cfg.combine_rows
    steps = t // n
    assert n % ROW_DMA_UNROLL == 0
    alpha = (2 * cfg.depth) ** 0.25
    pos_blocks = pos.reshape(k, steps, n).transpose(1, 0, 2)
    idx = lambda f: pl.BlockSpec((None, k, n), lambda i: (f(i), 0, 0), memory_space=pltpu.SMEM)
    row = pl.BlockSpec((n, d), lambda i: (i, 0))
    vec = pl.BlockSpec((1, d), lambda i: (0, 0))
    return pl.pallas_call(
        functools.partial(_combine_kernel, alpha=alpha, tile=cfg.dn_tn), grid=(steps,),
        in_specs=[idx(lambda i: i), idx(lambda i: jnp.minimum(i + 1, steps - 1)),
                  pl.BlockSpec(memory_space=pl.ANY), row,
                  pl.BlockSpec((n, k), lambda i: (i, 0)), vec, vec],
        out_specs=row,
        out_shape=jax.ShapeDtypeStruct((t, d), F32),
        scratch_shapes=[pltpu.VMEM((2, k, n, d // 2), jnp.int32), pltpu.SemaphoreType.DMA((2,))],
        compiler_params=_params(("arbitrary",)), name="moe_combine_ln2",
    )(pos_blocks, pos_blocks, ys, h1, gates_tk, g2.reshape(1, d), b2.reshape(1, d))


def _forward(cfg, x, ln_in_g, ln_in_b, w_in, conv_w, w_out, ln1_g, ln1_b, router_w, router_b,
             w_gate_up, b_gate_up, w_down, b_down, ln2_g, ln2_b):
    assert cfg.depth == 1
    b, s, d = x.shape
    x2d = x.reshape(b * s, d)
    slopes = jnp.asarray(_alibi_slopes(cfg.n_heads))
    h0 = _ln_in(x2d, ln_in_g, ln_in_b, cfg)
    proj = _matmul([h0], w_in[0], cfg, F32, "in_proj")
    attn = _attention(proj, slopes, cfg)
    y_conv = _gated_conv(proj, conv_w[0], cfg)
    mix = _matmul([attn, y_conv], w_out[0], cfg, F32, "out_proj")
    h1, h1_packed, top_idx, gates = _ln1_router(x2d, mix, ln_in_g, ln_in_b, ln1_g[0], ln1_b[0],
                                                router_w[0], router_b[0], cfg)
    row_tok, pos, *plan_arrays = _routing_plan(top_idx, cfg)
    n_tiles = cfg.d_expert // cfg.gu_tf
    xs, act0 = _gate_up(h1_packed, w_gate_up[0], b_gate_up[0], plan_arrays, cfg, 0, 1, row_tok=row_tok)
    acts = [act0]
    if n_tiles > 1:
        acts.append(_gate_up(xs, w_gate_up[0], b_gate_up[0], plan_arrays, cfg, 1, n_tiles - 1))
    ys = _down(acts, w_down[0], b_down[0], plan_arrays, cfg)
    out = _combine(ys, pos, gates.T, h1, ln2_g[0], ln2_b[0], cfg)
    return out.reshape(b, s, d)


_CONFIG = Config(
    batch=4, seq=2048, d_model=4096, head_dim=128, n_heads=24, conv_width=1024,
    patterns=((128, 1), (512, 4), (2048, 16)), n_experts=32, top_k=4, d_expert=2048, depth=1,
    ln_rows=256, mm_tm=1024, mm_tn=512, conv_cw=256, moe_rows=256, gu_tf=512, dn_tn=2048,
    gather_rows=256, combine_rows=64)


def kernel(x, ln_in_g, ln_in_b, w_in, conv_w, w_out, ln1_g, ln1_b, router_w, router_b,
           w_gate_up, b_gate_up, w_down, b_down, ln2_g, ln2_b):
    return _forward(_CONFIG, x, ln_in_g, ln_in_b, w_in, conv_w, w_out, ln1_g, ln1_b, router_w, router_b,
                    w_gate_up, b_gate_up, w_down, b_down, ln2_g, ln2_b)
```

```python
import dataclasses
import functools
import math

import numpy as np
import jax
import jax.numpy as jnp
from jax import lax
from jax.experimental import pallas as pl
from jax.experimental.pallas import tpu as pltpu

F32 = jnp.float32
BF16 = jnp.bfloat16

LN_EPS = 1e-5
SWIGLU_LIMIT = 7.0
SWIGLU_ALPHA = 1.702
MASKED_SCORE = -1e30
V7X_VMEM_LIMIT_BYTES = 56 * 1024 * 1024


@dataclasses.dataclass(frozen=True)
class Config:
    batch: int
    seq: int
    d_model: int
    head_dim: int
    n_heads: int
    conv_width: int
    patterns: tuple
    n_experts: int
    top_k: int
    d_expert: int
    depth: int
    ln_rows: int
    mm_tm: int
    mm_tn: int
    conv_cw: int
    moe_rows: int
    gu_tf: int
    dn_tn: int
    gather_rows: int
    combine_rows: int

    @property
    def attn_width(self):
        return self.n_heads * self.head_dim

    @property
    def tokens(self):
        return self.batch * self.seq


def _params(sem, vmem=V7X_VMEM_LIMIT_BYTES):
    return pltpu.CompilerParams(dimension_semantics=sem, vmem_limit_bytes=vmem)


def _alibi_slopes(n):
    def pow2(m):
        start = 2.0 ** (-(2.0 ** -(math.log2(m) - 3)))
        return [start ** (i + 1) for i in range(m)]
    if math.log2(n).is_integer():
        s = pow2(n)
    else:
        c = 2 ** int(math.floor(math.log2(n)))
        s = pow2(c) + pow2(2 * c)[0::2][: n - c]
    return np.asarray(s, dtype=np.float32)


def _pack_halves(x):
    n = x.shape[1] // 2
    return pltpu.pack_elementwise([x[:, :n], x[:, n:]], packed_dtype=BF16)


def _unpack_half(words, half):
    return pltpu.unpack_elementwise(words, index=half, packed_dtype=BF16, unpacked_dtype=F32)


def _layer_norm_rows(x, g, b):
    mu = jnp.mean(x, axis=-1, keepdims=True)
    xc = x - mu
    var = jnp.mean(xc * xc, axis=-1, keepdims=True)
    return xc * lax.rsqrt(var + LN_EPS) * g + b


def _ln_in_kernel(x_ref, g_ref, b_ref, o_ref):
    o_ref[...] = _layer_norm_rows(x_ref[...], g_ref[...], b_ref[...]).astype(o_ref.dtype)


def _ln_in(x2d, g, b, cfg):
    t, d = x2d.shape
    tm = cfg.ln_rows
    row = pl.BlockSpec((tm, d), lambda i: (i, 0))
    vec = pl.BlockSpec((1, d), lambda i: (0, 0))
    return pl.pallas_call(
        _ln_in_kernel, grid=(t // tm,), in_specs=[row, vec, vec], out_specs=row,
        out_shape=jax.ShapeDtypeStruct((t, d), BF16),
        compiler_params=_params(("parallel",)), name="ln_in",
    )(x2d, g.reshape(1, d), b.reshape(1, d))


def _matmul_kernel(*refs):
    *xw_refs, o_ref = refs
    n_in = len(xw_refs) // 2
    acc = None
    for x_ref, w_ref in zip(xw_refs[:n_in], xw_refs[n_in:]):
        part = jnp.dot(x_ref[...], w_ref[...].astype(BF16), preferred_element_type=F32)
        acc = part if acc is None else acc + part
    o_ref[...] = acc.astype(o_ref.dtype)


def _matmul(xs, w, cfg, out_dtype, name):
    m = xs[0].shape[0]
    n = w.shape[1]
    tm, tn = cfg.mm_tm, cfg.mm_tn
    x_specs, w_specs, row0 = [], [], 0
    for x in xs:
        k = x.shape[1]
        assert row0 % k == 0
        x_specs.append(pl.BlockSpec((tm, k), lambda i, j: (i, 0)))
        w_specs.append(pl.BlockSpec((k, tn), lambda i, j, r=row0 // k: (r, j)))
        row0 += k
    assert row0 == w.shape[0]
    return pl.pallas_call(
        _matmul_kernel, grid=(m // tm, n // tn),
        in_specs=x_specs + w_specs,
        out_specs=pl.BlockSpec((tm, tn), lambda i, j: (i, j)),
        out_shape=jax.ShapeDtypeStruct((m, n), out_dtype),
        compiler_params=_params(("parallel", "parallel")), name=name,
    )(*xs, *([w] * len(xs)))


def _attn_kernel(slopes_ref, q_ref, k_ref, v_ref, o_ref, acc_ref, m_ref, l_ref, *, cfg, blk):
    seq, hd = cfg.seq, cfg.head_dim
    nblk = seq // blk
    slope = slopes_ref[pl.program_id(1)]
    scale = 1.0 / math.sqrt(hd)

    qi = lax.broadcasted_iota(jnp.int32, (blk, blk), 0)
    ki = lax.broadcasted_iota(jnp.int32, (blk, blk), 1)
    d_own = qi - ki
    own_steps = d_own.astype(F32)
    prev_steps = (d_own + blk).astype(F32)

    for g, (window, dil) in enumerate(cfg.patterns):
        sub = seq // dil
        nb = sub // blk
        neg = -slope * float(dil)

        def to_blocks(ref, f, dil=dil, sub=sub, nb=nb):
            if dil == 1:
                return f(ref[...]).reshape(nblk, blk, hd)
            return jnp.concatenate(
                [f(ref[pl.ds(r, sub, stride=dil), :]).reshape(nb, blk, hd) for r in range(dil)], axis=0)

        def from_blocks(dst, x, g=g, dil=dil, sub=sub, nb=nb):
            if dil == 1:
                dst[g] = x.reshape(seq, hd)
            else:
                for r in range(dil):
                    dst[g, pl.ds(r, sub, stride=dil), :] = x[r * nb:(r + 1) * nb].reshape(sub, hd)

        q = to_blocks(q_ref, lambda x: (x * scale).astype(BF16))
        keys = to_blocks(k_ref, lambda x: x.astype(BF16))
        vals = to_blocks(v_ref, lambda x: x.astype(BF16))
        bias = jnp.where(d_own >= 0, own_steps * neg, MASKED_SCORE)[None]
        if nb > 1:
            def with_prev(x, dil=dil, nb=nb):
                x4 = x.reshape(dil, nb, blk, hd)
                prev = jnp.concatenate([jnp.zeros((dil, 1, blk, hd), x.dtype), x4[:, :-1]], axis=1)
                return jnp.concatenate([prev.reshape(nblk, blk, hd), x], axis=1)
            keys, vals = with_prev(keys), with_prev(vals)
            n_of_block = lax.broadcasted_iota(jnp.int32, (nblk, blk, blk), 0) & (nb - 1)
            prev_ok = jnp.logical_and(d_own[None] <= 0, n_of_block > 0)
            bias_prev = jnp.where(prev_ok, (prev_steps * neg)[None], MASKED_SCORE)
            bias = jnp.concatenate([bias_prev, jnp.broadcast_to(bias, (nblk, blk, blk))], axis=2)
        s = jnp.einsum("bqd,bkd->bqk", q, keys, preferred_element_type=F32) + bias
        m = jnp.max(s, axis=-1, keepdims=True)
        p = jnp.exp(s - m).astype(BF16)
        vals1 = jnp.concatenate([vals, jnp.ones(vals.shape, BF16)], axis=-1)
        pv = jnp.einsum("bqk,bkd->bqd", p, vals1, preferred_element_type=F32)
        from_blocks(acc_ref, pv[:, :, :hd])
        from_blocks(l_ref, pv[:, :, hd:])
        from_blocks(m_ref, jnp.broadcast_to(m, (nblk, blk, hd)))

    n_win = len(cfg.patterns)
    gs = max(c for c in range(1, 5) if nblk % c == 0)

    def merge(i, carry):
        for j in range(gs):
            r = pl.ds(pl.multiple_of((i * gs + j) * blk, blk), blk)
            ms = [m_ref[g, r, :] for g in range(n_win)]
            m_all = functools.reduce(jnp.maximum, ms)
            ws = [jnp.exp(m - m_all) for m in ms]
            num = functools.reduce(lambda a, b: a + b, [w * acc_ref[g, r, :] for g, w in enumerate(ws)])
            den = functools.reduce(lambda a, b: a + b, [w * l_ref[g, r, :] for g, w in enumerate(ws)])
            o_ref[r, :] = (num / den).astype(o_ref.dtype)
        return carry
    lax.fori_loop(0, nblk // gs, merge, 0)


def _attention(proj, slopes, cfg):
    seq, hd, nh = cfg.seq, cfg.head_dim, cfg.n_heads
    blks = {w // d for w, d in cfg.patterns}
    assert len(blks) == 1, "all windows must share one block length"
    blk = blks.pop()
    for w, d in cfg.patterns:
        nb = seq // (d * blk)
        assert d & (d - 1) == 0 and nb & (nb - 1) == 0 and nb * d * blk == seq
    grid_spec = pltpu.PrefetchScalarGridSpec(
        num_scalar_prefetch=1, grid=(cfg.batch, nh),
        in_specs=[pl.BlockSpec((seq, hd), lambda b, h, s: (b, h)),
                  pl.BlockSpec((seq, hd), lambda b, h, s: (b, nh + h)),
                  pl.BlockSpec((seq, hd), lambda b, h, s: (b, 2 * nh + h))],
        out_specs=pl.BlockSpec((seq, hd), lambda b, h, s: (b, h)),
        scratch_shapes=[pltpu.VMEM((len(cfg.patterns), seq, hd), F32)] * 3)
    return pl.pallas_call(
        functools.partial(_attn_kernel, cfg=cfg, blk=blk), grid_spec=grid_spec,
        out_shape=jax.ShapeDtypeStruct((cfg.tokens, cfg.attn_width), BF16),
        compiler_params=_params(("parallel", "parallel")), name="dilated_attention",
    )(slopes, proj, proj, proj)


def _conv_kernel(gb_ref, gc_ref, val_ref, w_ref, o_ref):
    u = gc_ref[...] * val_ref[...]
    row = lax.broadcasted_iota(jnp.int32, u.shape, 0)
    u1 = jnp.where(row >= 1, pltpu.roll(u, 1, 0), 0.0)
    u2 = jnp.where(row >= 2, pltpu.roll(u, 2, 0), 0.0)
    w = w_ref[...]
    z = w[0:1, :] * u2 + w[1:2, :] * u1 + w[2:3, :] * u
    o_ref[...] = (gb_ref[...] * z).astype(o_ref.dtype)


def _gated_conv(proj, conv_w, cfg):
    seq, cw, c = cfg.seq, cfg.conv_cw, cfg.conv_width
    base = 3 * cfg.attn_width // cw
    per = c // cw
    spec = lambda off: pl.BlockSpec((seq, cw), lambda b, j, off=off: (b, base + off * per + j))
    return pl.pallas_call(
        _conv_kernel, grid=(cfg.batch, per),
        in_specs=[spec(0), spec(1), spec(2), pl.BlockSpec((conv_w.shape[0], cw), lambda b, j: (0, j))],
        out_specs=pl.BlockSpec((seq, cw), lambda b, j: (b, j)),
        out_shape=jax.ShapeDtypeStruct((cfg.tokens, c), BF16),
        compiler_params=_params(("parallel", "parallel")), name="gated_conv",
    )(proj, proj, proj, conv_w)


def _ln1_router_kernel(x_ref, mix_ref, gi_ref, bi_ref, g1_ref, b1_ref, rw_ref, rb_ref,
                       h_ref, hp_ref, idx_ref, gate_ref, *, alpha, top_k):
    h0 = _layer_norm_rows(x_ref[...], gi_ref[...], bi_ref[...])
    h1 = _layer_norm_rows(alpha * h0 + mix_ref[...], g1_ref[...], b1_ref[...])
    h_ref[...] = h1
    hp_ref[...] = _pack_halves(h1)
    def split(v):
        hi = v.astype(BF16)
        return hi, (v - hi.astype(F32)).astype(BF16)

    def dot_t(a, b):
        return lax.dot_general(a, b, (((1,), (1,)), ((), ())), preferred_element_type=F32)
    w_hi, w_lo = split(rw_ref[...])
    h_hi, h_lo = split(h1)
    logits = dot_t(w_hi, h_hi) + (dot_t(w_hi, h_lo) + dot_t(w_lo, h_hi)) + rb_ref[...]
    n_e = logits.shape[0]
    eidx = lax.broadcasted_iota(jnp.int32, logits.shape, 0)
    vals, idxs = [], []
    for _ in range(top_k):
        m = jnp.max(logits, axis=0, keepdims=True)
        i = jnp.min(jnp.where(logits == m, eidx, n_e), axis=0, keepdims=True)
        vals.append(m)
        idxs.append(i)
        logits = jnp.where(eidx == i, -jnp.inf, logits)
    ex = [jnp.exp(v - vals[0]) for v in vals]
    den = functools.reduce(lambda a, b: a + b, ex)
    idx_ref[...] = jnp.concatenate(idxs, axis=0)
    gate_ref[...] = jnp.concatenate([e / den for e in ex], axis=0)


def _ln1_router(x2d, mix, gi, bi, g1, b1, router_w, router_b, cfg):
    t, d = x2d.shape
    e, k = cfg.n_experts, cfg.top_k
    tm = cfg.ln_rows
    alpha = (2 * cfg.depth) ** 0.25
    row = pl.BlockSpec((tm, d), lambda i: (i, 0))
    vec = pl.BlockSpec((1, d), lambda i: (0, 0))
    sel = pl.BlockSpec((k, tm), lambda i: (0, i))
    return pl.pallas_call(
        functools.partial(_ln1_router_kernel, alpha=alpha, top_k=k), grid=(t // tm,),
        in_specs=[row, row, vec, vec, vec, vec,
                  pl.BlockSpec((e, d), lambda i: (0, 0)), pl.BlockSpec((e, 1), lambda i: (0, 0))],
        out_specs=[row, pl.BlockSpec((tm, d // 2), lambda i: (i, 0)), sel, sel],
        out_shape=[jax.ShapeDtypeStruct((t, d), F32), jax.ShapeDtypeStruct((t, d // 2), jnp.int32),
                   jax.ShapeDtypeStruct((k, t), jnp.int32),
                   jax.ShapeDtypeStruct((k, t), F32)],
        compiler_params=_params(("parallel",)), name="ln1_router",
    )(x2d, mix, gi.reshape(1, d), bi.reshape(1, d), g1.reshape(1, d), b1.reshape(1, d),
      router_w.T, router_b.reshape(e, 1))


def _routing_plan(top_idx, cfg):
    k, t = top_idx.shape
    e, rb = cfg.n_experts, cfg.moe_rows
    a = k * t
    nb = a // rb + e
    p = nb * rb
    i32 = jnp.int32
    flat_e = top_idx.reshape(a)
    slot = jnp.arange(a, dtype=i32)
    experts = jnp.arange(e, dtype=i32)
    _, order = lax.sort((flat_e, slot), num_keys=1)
    counts = jnp.sum((flat_e[:, None] == experts[None, :]).astype(i32), axis=0)
    ends = jnp.cumsum(counts)
    padded = (counts + rb - 1) // rb * rb
    pends = jnp.cumsum(padded)
    pstarts = pends - padded
    gap = padded - counts

    def gaps_before(x, bounds):
        return jnp.sum(jnp.where(x[:, None] >= bounds[None, :], gap[None, :], 0), axis=1)
    pos_sorted = slot + gaps_before(slot, ends)
    _, pos = lax.sort((order, pos_sorted), num_keys=1)

    rows = jnp.arange(p, dtype=i32)
    src = rows - gaps_before(rows, pends)
    seg_end = jnp.sum(jnp.where(rows[:, None] >= pstarts[None, :], counts[None, :], 0), axis=1)
    row_tok = jnp.where(src < seg_end, order[jnp.clip(src, 0, a - 1)] % t, 0).astype(i32)

    n_blocks = pends[-1] // rb
    bi = jnp.arange(nb, dtype=i32)
    raw_e = jnp.minimum(jnp.sum((bi[:, None] * rb >= pends[None, :]).astype(i32), axis=1), e - 1)
    block_e = jnp.where(bi < n_blocks, raw_e, raw_e[n_blocks - 1])
    first = jnp.concatenate([jnp.ones((1,), i32), (block_e[1:] != block_e[:-1]).astype(i32)])
    tile_ord = jnp.cumsum(first) - 1
    tile_e = jnp.max(jnp.where(tile_ord[None, :] == bi[:, None], block_e[None, :], -1), axis=1)
    next_e = tile_e[jnp.minimum(tile_ord + 1, nb - 1)]
    meta = jnp.stack([n_blocks, tile_ord[-1] + 1]).astype(i32)
    return (row_tok, pos.reshape(k, t).astype(i32), block_e.astype(i32), first, tile_ord.astype(i32),
            next_e.astype(i32), meta)


ROW_DMA_UNROLL = 8


def _row_copy(src_hbm, dst_vmem, sem, src_row, dst_row):
    return pltpu.make_async_copy(src_hbm.at[pl.ds(src_row, 1), :], dst_vmem.at[pl.ds(dst_row, 1), :], sem)


def _start_rows(src_hbm, dst_vmem, sem, n, src_row_of, inline=False, queues=(0, 1)):
    if inline:
        for r in range(n):
            _row_copy(src_hbm, dst_vmem, sem, src_row_of(r), r).start(priority=queues[r % len(queues)])
        return

    def body(c, carry):
        for u in range(ROW_DMA_UNROLL):
            r = c * ROW_DMA_UNROLL + u
            _row_copy(src_hbm, dst_vmem, sem, src_row_of(r), r).start(priority=queues[u % len(queues)])
        return carry
    lax.fori_loop(0, n // ROW_DMA_UNROLL, body, 0)


def _wait_rows(src_hbm, dst_vmem, sem, n):
    def body(c, carry):
        for u in range(ROW_DMA_UNROLL):
            _row_copy(src_hbm, dst_vmem, sem, 0, c * ROW_DMA_UNROLL + u).wait()
        return carry
    lax.fori_loop(0, n // ROW_DMA_UNROLL, body, 0)


WEIGHT_CAST_CHUNKS = 4
WEIGHT_DMA_PRIORITY = 1
ROW_SLOTS = 3

class _TilePlan:
    def __init__(self, be_ref, first_ref, ord_ref, next_ref, meta_ref):
        j, i = pl.program_id(0), pl.program_id(1)
        self.j, self.i = j, i
        self.expert = be_ref[i]
        self.first = first_ref[i] == 1
        self.valid = i < meta_ref[0]
        tile = j * meta_ref[1] + ord_ref[i]
        self.slot = tile & 1
        self.is_tile0 = tile == 0
        nxt = next_ref[i]
        more_j = j + 1 < pl.num_programs(0)
        self.has_next = jnp.logical_or(nxt >= 0, more_j)
        self.next_e = jnp.where(nxt >= 0, nxt, be_ref[0])
        self.next_j = jnp.where(nxt >= 0, j, j + 1)


def _first_step_dot(x, stage, w_s):
    k = x.shape[1]
    ck = k // WEIGHT_CAST_CHUNKS
    acc = None
    for c in range(WEIGHT_CAST_CHUNKS):
        wb = stage[pl.ds(c * ck, ck), :].astype(BF16)
        w_s[pl.ds(c * ck, ck), :] = wb
        part = jnp.dot(x[:, c * ck:(c + 1) * ck], wb, preferred_element_type=F32)
        acc = part if acc is None else acc + part
    return acc


def _swiglu(g, u):
    g = jnp.minimum(g, SWIGLU_LIMIT)
    u = jnp.clip(u, -SWIGLU_LIMIT, SWIGLU_LIMIT)
    return (u + 1.0) * (g * jax.nn.sigmoid(SWIGLU_ALPHA * g))


def _gate_up_kernel(be_ref, first_ref, ord_ref, next_ref, meta_ref, *refs, tf, f, j0, gather):
    if gather:
        (tok_ref, tok1_ref, tok2_ref, h_hbm, w_hbm, bg_ref, bu_ref, x_ref, o_ref,
         rows, stage_g, stage_u, wg_s, wu_s, row_sem, sem) = refs
    else:
        x_ref, w_hbm, bg_ref, bu_ref, o_ref, stage_g, stage_u, wg_s, wu_s, sem = refs
    plan = _TilePlan(be_ref, first_ref, ord_ref, next_ref, meta_ref)

    if gather:
        n, half = rows.shape[1], rows.shape[2]
        live = meta_ref[0]
        slot = lax.rem(plan.i, ROW_SLOTS)
        ahead = lax.rem(plan.i + 2, ROW_SLOTS)

        @pl.when(plan.i == 0)
        def _():
            _start_rows(h_hbm, rows.at[0], row_sem.at[0], n, lambda r: tok_ref[0, r])

        @pl.when(jnp.logical_and(plan.i == 0, live > 1))
        def _():
            _start_rows(h_hbm, rows.at[1], row_sem.at[1], n, lambda r: tok1_ref[0, r])

        @pl.when(plan.i + 2 < live)
        def _():
            _start_rows(h_hbm, rows.at[ahead], row_sem.at[ahead], n, lambda r: tok2_ref[0, r])

        @pl.when(plan.valid)
        def _():
            _wait_rows(h_hbm, rows.at[slot], row_sem.at[slot], n)
            words = rows[slot]
            x_ref[:, :half] = _unpack_half(words, 0).astype(x_ref.dtype)
            x_ref[:, half:] = _unpack_half(words, 1).astype(x_ref.dtype)

        @pl.when(jnp.logical_not(plan.valid))
        def _():
            x_ref[...] = jnp.zeros(x_ref.shape, x_ref.dtype)

    def tile_copies(e, j, slot):
        col = pl.multiple_of((j + j0) * tf, tf)
        return (pltpu.make_async_copy(w_hbm.at[e, :, pl.ds(col, tf)], stage_g.at[slot], sem.at[slot]),
                pltpu.make_async_copy(w_hbm.at[e, :, pl.ds(f + col, tf)], stage_u.at[slot], sem.at[slot]))

    @pl.when(plan.first)
    def _():
        @pl.when(plan.is_tile0)
        def _():
            for c in tile_copies(plan.expert, plan.j, plan.slot):
                c.start(priority=WEIGHT_DMA_PRIORITY)
        for c in tile_copies(plan.expert, plan.j, plan.slot):
            c.wait()

        @pl.when(plan.has_next)
        def _():
            for c in tile_copies(plan.next_e, plan.next_j, 1 - plan.slot):
                c.start(priority=WEIGHT_DMA_PRIORITY)
        x = x_ref[...]
        g = _first_step_dot(x, stage_g.at[plan.slot], wg_s) + bg_ref[...]
        u = _first_step_dot(x, stage_u.at[plan.slot], wu_s) + bu_ref[...]
        o_ref[...] = _swiglu(g, u).astype(o_ref.dtype)

    @pl.when(jnp.logical_and(jnp.logical_not(plan.first), plan.valid))
    def _():
        x = x_ref[...]
        g = jnp.dot(x, wg_s[...], preferred_element_type=F32) + bg_ref[...]
        u = jnp.dot(x, wu_s[...], preferred_element_type=F32) + bu_ref[...]
        o_ref[...] = _swiglu(g, u).astype(o_ref.dtype)

    @pl.when(jnp.logical_not(plan.valid))
    def _():
        o_ref[...] = jnp.zeros(o_ref.shape, o_ref.dtype)


def _gate_up(x_src, w_gu, b_gu, plan_arrays, cfg, j0, nj, row_tok=None):
    gather = row_tok is not None
    e, f, rb, tf = cfg.n_experts, cfg.d_expert, cfg.moe_rows, cfg.gu_tf
    d = cfg.d_model
    nf = f // tf
    p = row_tok.shape[0] if gather else x_src.shape[0]
    nb = p // rb
    assert d % (WEIGHT_CAST_CHUNKS * 128) == 0 and rb % ROW_DMA_UNROLL == 0
    bias = lambda off: pl.BlockSpec((None, 1, tf), lambda j, i, be, *_: (be[i], 0, off + j0 + j))
    act_spec = pl.BlockSpec((rb, tf), lambda j, i, be, *_: (i, j))
    row_spec = pl.BlockSpec((rb, d), lambda j, i, be, *_: (i, 0))
    weights = [pltpu.VMEM((2, d, tf), F32), pltpu.VMEM((2, d, tf), F32),
               pltpu.VMEM((d, tf), BF16), pltpu.VMEM((d, tf), BF16)]
    act_shape = jax.ShapeDtypeStruct((p, nj * tf), BF16)
    b3 = b_gu.reshape(e, 1, 2 * f)
    if gather:
        assert nj == 1
        tok = row_tok.reshape(nb, 1, rb)
        idx = lambda g: pl.BlockSpec((None, 1, rb), lambda j, i, *_: (g(i), 0, 0), memory_space=pltpu.SMEM)
        in_specs = [idx(lambda i: i), idx(lambda i: jnp.minimum(i + 1, nb - 1)),
                    idx(lambda i: jnp.minimum(i + 2, nb - 1)),
                    pl.BlockSpec(memory_space=pl.ANY), pl.BlockSpec(memory_space=pl.ANY), bias(0), bias(nf)]
        out_specs = [row_spec, act_spec]
        out_shape = [jax.ShapeDtypeStruct((p, d), BF16), act_shape]
        scratch = ([pltpu.VMEM((ROW_SLOTS, rb, d // 2), jnp.int32)] + weights
                   + [pltpu.SemaphoreType.DMA((ROW_SLOTS,)), pltpu.SemaphoreType.DMA((2,))])
        operands = (tok, tok, tok, x_src, w_gu, b3, b3)
    else:
        in_specs = [row_spec, pl.BlockSpec(memory_space=pl.ANY), bias(0), bias(nf)]
        out_specs, out_shape = act_spec, act_shape
        scratch = weights + [pltpu.SemaphoreType.DMA((2,))]
        operands = (x_src, w_gu, b3, b3)
    grid_spec = pltpu.PrefetchScalarGridSpec(
        num_scalar_prefetch=5, grid=(nj, nb), in_specs=in_specs, out_specs=out_specs, scratch_shapes=scratch)
    return pl.pallas_call(
        functools.partial(_gate_up_kernel, tf=tf, f=f, j0=j0, gather=gather), grid_spec=grid_spec,
        out_shape=out_shape, compiler_params=_params(("arbitrary", "arbitrary")),
        name="moe_gather_gate_up" if gather else "moe_gate_up",
    )(*plan_arrays, *operands)


def _down_kernel(be_ref, first_ref, ord_ref, next_ref, meta_ref, *refs, tn):
    *a_refs, w_hbm, b_ref, o_ref, stage, w_s, sem = refs
    plan = _TilePlan(be_ref, first_ref, ord_ref, next_ref, meta_ref)

    def act():
        return jnp.concatenate([r[...] for r in a_refs], axis=1) if len(a_refs) > 1 else a_refs[0][...]

    def tile_copy(e, j, slot):
        col = pl.multiple_of(j * tn, tn)
        return pltpu.make_async_copy(w_hbm.at[e, :, pl.ds(col, tn)], stage.at[slot], sem.at[slot])

    @pl.when(plan.first)
    def _():
        @pl.when(plan.is_tile0)
        def _():
            tile_copy(plan.expert, plan.j, plan.slot).start(priority=WEIGHT_DMA_PRIORITY)
        tile_copy(plan.expert, plan.j, plan.slot).wait()

        @pl.when(plan.has_next)
        def _():
            tile_copy(plan.next_e, plan.next_j, 1 - plan.slot).start(priority=WEIGHT_DMA_PRIORITY)
        o_ref[...] = _pack_halves(_first_step_dot(act(), stage.at[plan.slot], w_s) + b_ref[...])

    @pl.when(jnp.logical_and(jnp.logical_not(plan.first), plan.valid))
    def _():
        o_ref[...] = _pack_halves(jnp.dot(act(), w_s[...], preferred_element_type=F32) + b_ref[...])

    @pl.when(jnp.logical_not(plan.valid))
    def _():
        o_ref[...] = jnp.zeros(o_ref.shape, o_ref.dtype)


def _down(acts, w_dn, b_dn, plan_arrays, cfg):
    p = acts[0].shape[0]
    f = sum(a.shape[1] for a in acts)
    e, d, rb, tn = cfg.n_experts, cfg.d_model, cfg.moe_rows, cfg.dn_tn
    assert f == cfg.d_expert and f % (WEIGHT_CAST_CHUNKS * 128) == 0
    grid_spec = pltpu.PrefetchScalarGridSpec(
        num_scalar_prefetch=5, grid=(d // tn, p // rb),
        in_specs=[pl.BlockSpec((rb, a.shape[1]), lambda j, i, be, *_: (i, 0)) for a in acts] + [
                  pl.BlockSpec(memory_space=pl.ANY),
                  pl.BlockSpec((None, 1, tn), lambda j, i, be, *_: (be[i], 0, j))],
        out_specs=pl.BlockSpec((rb, tn // 2), lambda j, i, be, *_: (i, j)),
        scratch_shapes=[pltpu.VMEM((2, f, tn), F32), pltpu.VMEM((f, tn), BF16),
                        pltpu.SemaphoreType.DMA((2,))])
    return pl.pallas_call(
        functools.partial(_down_kernel, tn=tn), grid_spec=grid_spec,
        out_shape=jax.ShapeDtypeStruct((p, d // 2), jnp.int32),
        compiler_params=_params(("arbitrary", "arbitrary")), name="moe_down",
    )(*plan_arrays, *acts, w_dn, b_dn.reshape(e, 1, d))


def _combine_kernel(pos_ref, pos_next_ref, ys_hbm, h_ref, gate_ref, g2_ref, b2_ref, o_ref, buf, sem, *,
                    alpha, tile):
    i = pl.program_id(0)
    top_k, n = buf.shape[1], buf.shape[2]
    slot = i % 2
    hw = tile // 2

    def start(idx_ref, s, inline):
        for k in range(top_k):
            _start_rows(ys_hbm, buf.at[s, k], sem.at[s], n, lambda t, k=k: idx_ref[k, t], inline=inline)

    def wait(s):
        for k in range(top_k):
            _wait_rows(ys_hbm, buf.at[s, k], sem.at[s], n)

    @pl.when(i == 0)
    def _():
        start(pos_ref, 0, inline=False)

    wait(slot)
    start(pos_next_ref, 1 - slot, inline=True)
    gates = gate_ref[...]
    slabs = []
    for j in range(buf.shape[3] // hw):
        words = [buf[slot, k, :, j * hw:(j + 1) * hw] for k in range(top_k)]
        for half in (0, 1):
            terms = [gates[:, k:k + 1] * _unpack_half(words[k], half) for k in range(top_k)]
            slabs.append(functools.reduce(lambda a, b: a + b, terms))
    y = jnp.concatenate(slabs, axis=1)
    o_ref[...] = _layer_norm_rows(alpha * h_ref[...] + y, g2_ref[...], b2_ref[...])

    @pl.when(i == pl.num_programs(0) - 1)
    def _():
        wait(1 - slot)


def _combine(ys, pos, gates_tk, h1, g2, b2, cfg):
    t, d = h1.shape
    k, n = cfg.top_k, cfg.combine_rows
    steps = t // n
    assert n % ROW_DMA_UNROLL == 0
    alpha = (2 * cfg.depth) ** 0.25
    pos_blocks = pos.reshape(k, steps, n).transpose(1, 0, 2)
    idx = lambda f: pl.BlockSpec((None, k, n), lambda i: (f(i), 0, 0), memory_space=pltpu.SMEM)
    row = pl.BlockSpec((n, d), lambda i: (i, 0))
    vec = pl.BlockSpec((1, d), lambda i: (0, 0))
    return pl.pallas_call(
        functools.partial(_combine_kernel, alpha=alpha, tile=cfg.dn_tn), grid=(steps,),
        in_specs=[idx(lambda i: i), idx(lambda i: jnp.minimum(i + 1, steps - 1)),
                  pl.BlockSpec(memory_space=pl.ANY), row,
                  pl.BlockSpec((n, k), lambda i: (i, 0)), vec, vec],
        out_specs=row,
        out_shape=jax.ShapeDtypeStruct((t, d), F32),
        scratch_shapes=[pltpu.VMEM((2, k, n, d // 2), jnp.int32), pltpu.SemaphoreType.DMA((2,))],
        compiler_params=_params(("arbitrary",)), name="moe_combine_ln2",
    )(pos_blocks, pos_blocks, ys, h1, gates_tk, g2.reshape(1, d), b2.reshape(1, d))


def _forward(cfg, x, ln_in_g, ln_in_b, w_in, conv_w, w_out, ln1_g, ln1_b, router_w, router_b,
             w_gate_up, b_gate_up, w_down, b_down, ln2_g, ln2_b):
    assert cfg.depth == 1
    b, s, d = x.shape
    x2d = x.reshape(b * s, d)
    slopes = jnp.asarray(_alibi_slopes(cfg.n_heads))
    h0 = _ln_in(x2d, ln_in_g, ln_in_b, cfg)
    proj = _matmul([h0], w_in[0], cfg, F32, "in_proj")
    attn = _attention(proj, slopes, cfg)
    y_conv = _gated_conv(proj, conv_w[0], cfg)
    mix = _matmul([attn, y_conv], w_out[0], cfg, F32, "out_proj")
    h1, h1_packed, top_idx, gates = _ln1_router(x2d, mix, ln_in_g, ln_in_b, ln1_g[0], ln1_b[0],
                                                router_w[0], router_b[0], cfg)
    row_tok, pos, *plan_arrays = _routing_plan(top_idx, cfg)
    n_tiles = cfg.d_expert // cfg.gu_tf
    xs, act0 = _gate_up(h1_packed, w_gate_up[0], b_gate_up[0], plan_arrays, cfg, 0, 1, row_tok=row_tok)
    acts = [act0]
    if n_tiles > 1:
        acts.append(_gate_up(xs, w_gate_up[0], b_gate_up[0], plan_arrays, cfg, 1, n_tiles - 1))
    ys = _down(acts, w_down[0], b_down[0], plan_arrays, cfg)
    out = _combine(ys, pos, gates.T, h1, ln2_g[0], ln2_b[0], cfg)
    return out.reshape(b, s, d)


_CONFIG = Config(
    batch=4, seq=2048, d_model=4096, head_dim=128, n_heads=24, conv_width=1024,
    patterns=((128, 1), (512, 4), (2048, 16)), n_experts=32, top_k=4, d_expert=2048, depth=1,
    ln_rows=256, mm_tm=1024, mm_tn=512, conv_cw=256, moe_rows=256, gu_tf=512, dn_tn=2048,
    gather_rows=256, combine_rows=64)


def kernel(x, ln_in_g, ln_in_b, w_in, conv_w, w_out, ln1_g, ln1_b, router_w, router_b,
           w_gate_up, b_gate_up, w_down, b_down, ln2_g, ln2_b):
    return _forward(_CONFIG, x, ln_in_g, ln_in_b, w_in, conv_w, w_out, ln1_g, ln1_b, router_w, router_b,
                    w_gate_up, b_gate_up, w_down, b_down, ln2_g, ln2_b)
```

```python
import dataclasses
import functools
import math

import numpy as np
import jax
import jax.numpy as jnp
from jax import lax
from jax.experimental import pallas as pl
from jax.experimental.pallas import tpu as pltpu

F32 = jnp.float32
BF16 = jnp.bfloat16

LN_EPS = 1e-5
SWIGLU_LIMIT = 7.0
SWIGLU_ALPHA = 1.702
MASKED_SCORE = -1e30
V7X_VMEM_LIMIT_BYTES = 56 * 1024 * 1024


@dataclasses.dataclass(frozen=True)
class Config:
    batch: int
    seq: int
    d_model: int
    head_dim: int
    n_heads: int
    conv_width: int
    patterns: tuple
    n_experts: int
    top_k: int
    d_expert: int
    depth: int
    ln_rows: int
    mm_tm: int
    mm_tn: int
    conv_cw: int
    moe_rows: int
    gu_tf: int
    dn_tn: int
    gather_rows: int
    combine_rows: int

    @property
    def attn_width(self):
        return self.n_heads * self.head_dim

    @property
    def tokens(self):
        return self.batch * self.seq


def _params(sem, vmem=V7X_VMEM_LIMIT_BYTES):
    return pltpu.CompilerParams(dimension_semantics=sem, vmem_limit_bytes=vmem)


def _alibi_slopes(n):
    def pow2(m):
        start = 2.0 ** (-(2.0 ** -(math.log2(m) - 3)))
        return [start ** (i + 1) for i in range(m)]
    if math.log2(n).is_integer():
        s = pow2(n)
    else:
        c = 2 ** int(math.floor(math.log2(n)))
        s = pow2(c) + pow2(2 * c)[0::2][: n - c]
    return np.asarray(s, dtype=np.float32)


def _pack_halves(x):
    n = x.shape[1] // 2
    return pltpu.pack_elementwise([x[:, :n], x[:, n:]], packed_dtype=BF16)


def _unpack_half(words, half):
    return pltpu.unpack_elementwise(words, index=half, packed_dtype=BF16, unpacked_dtype=F32)


def _layer_norm_rows(x, g, b):
    mu = jnp.mean(x, axis=-1, keepdims=True)
    xc = x - mu
    var = jnp.mean(xc * xc, axis=-1, keepdims=True)
    return xc * lax.rsqrt(var + LN_EPS) * g + b


def _ln_in_kernel(x_ref, g_ref, b_ref, o_ref):
    o_ref[...] = _layer_norm_rows(x_ref[...], g_ref[...], b_ref[...]).astype(o_ref.dtype)


def _ln_in(x2d, g, b, cfg):
    t, d = x2d.shape
    tm = cfg.ln_rows
    row = pl.BlockSpec((tm, d), lambda i: (i, 0))
    vec = pl.BlockSpec((1, d), lambda i: (0, 0))
    return pl.pallas_call(
        _ln_in_kernel, grid=(t // tm,), in_specs=[row, vec, vec], out_specs=row,
        out_shape=jax.ShapeDtypeStruct((t, d), BF16),
        compiler_params=_params(("parallel",)), name="ln_in",
    )(x2d, g.reshape(1, d), b.reshape(1, d))


def _matmul_kernel(*refs):
    *xw_refs, o_ref = refs
    n_in = len(xw_refs) // 2
    acc = None
    for x_ref, w_ref in zip(xw_refs[:n_in], xw_refs[n_in:]):
        part = jnp.dot(x_ref[...], w_ref[...].astype(BF16), preferred_element_type=F32)
        acc = part if acc is None else acc + part
    o_ref[...] = acc.astype(o_ref.dtype)


def _matmul(xs, w, cfg, out_dtype, name):
    m = xs[0].shape[0]
    n = w.shape[1]
    tm, tn = cfg.mm_tm, cfg.mm_tn
    x_specs, w_specs, row0 = [], [], 0
    for x in xs:
        k = x.shape[1]
        assert row0 % k == 0
        x_specs.append(pl.BlockSpec((tm, k), lambda i, j: (i, 0)))
        w_specs.append(pl.BlockSpec((k, tn), lambda i, j, r=row0 // k: (r, j)))
        row0 += k
    assert row0 == w.shape[0]
    return pl.pallas_call(
        _matmul_kernel, grid=(m // tm, n // tn),
        in_specs=x_specs + w_specs,
        out_specs=pl.BlockSpec((tm, tn), lambda i, j: (i, j)),
        out_shape=jax.ShapeDtypeStruct((m, n), out_dtype),
        compiler_params=_params(("parallel", "parallel")), name=name,
    )(*xs, *([w] * len(xs)))


def _attn_kernel(slopes_ref, q_ref, k_ref, v_ref, o_ref, acc_ref, m_ref, l_ref, *, cfg, blk):
    seq, hd = cfg.seq, cfg.head_dim
    nblk = seq // blk
    slope = slopes_ref[pl.program_id(1)]
    scale = 1.0 / math.sqrt(hd)

    qi = lax.broadcasted_iota(jnp.int32, (blk, blk), 0)
    ki = lax.broadcasted_iota(jnp.int32, (blk, blk), 1)
    d_own = qi - ki
    own_steps = d_own.astype(F32)
    prev_steps = (d_own + blk).astype(F32)

    for g, (window, dil) in enumerate(cfg.patterns):
        sub = seq // dil
        nb = sub // blk
        neg = -slope * float(dil)

        def to_blocks(ref, f, dil=dil, sub=sub, nb=nb):
            if dil == 1:
                return f(ref[...]).reshape(nblk, blk, hd)
            return jnp.concatenate(
                [f(ref[pl.ds(r, sub, stride=dil), :]).reshape(nb, blk, hd) for r in range(dil)], axis=0)

        def from_blocks(dst, x, g=g, dil=dil, sub=sub, nb=nb):
            if dil == 1:
                dst[g] = x.reshape(seq, hd)
            else:
                for r in range(dil):
                    dst[g, pl.ds(r, sub, stride=dil), :] = x[r * nb:(r + 1) * nb].reshape(sub, hd)

        q = to_blocks(q_ref, lambda x: (x * scale).astype(BF16))
        keys = to_blocks(k_ref, lambda x: x.astype(BF16))
        vals = to_blocks(v_ref, lambda x: x.astype(BF16))
        bias = jnp.where(d_own >= 0, own_steps * neg, MASKED_SCORE)[None]
        if nb > 1:
            def with_prev(x, dil=dil, nb=nb):
                x4 = x.reshape(dil, nb, blk, hd)
                prev = jnp.concatenate([jnp.zeros((dil, 1, blk, hd), x.dtype), x4[:, :-1]], axis=1)
                return jnp.concatenate([prev.reshape(nblk, blk, hd), x], axis=1)
            keys, vals = with_prev(keys), with_prev(vals)
            n_of_block = lax.broadcasted_iota(jnp.int32, (nblk, blk, blk), 0) & (nb - 1)
            prev_ok = jnp.logical_and(d_own[None] <= 0, n_of_block > 0)
            bias_prev = jnp.where(prev_ok, (prev_steps * neg)[None], MASKED_SCORE)
            bias = jnp.concatenate([bias_prev, jnp.broadcast_to(bias, (nblk, blk, blk))], axis=2)
        s = jnp.einsum("bqd,bkd->bqk", q, keys, preferred_element_type=F32) + bias
        m = jnp.max(s, axis=-1, keepdims=True)
        p = jnp.exp(s - m).astype(BF16)
        vals1 = jnp.concatenate([vals, jnp.ones(vals.shape, BF16)], axis=-1)
        pv = jnp.einsum("bqk,bkd->bqd", p, vals1, preferred_element_type=F32)
        from_blocks(acc_ref, pv[:, :, :hd])
        from_blocks(l_ref, pv[:, :, hd:])
        from_blocks(m_ref, jnp.broadcast_to(m, (nblk, blk, hd)))

    n_win = len(cfg.patterns)
    gs = max(c for c in range(1, 5) if nblk % c == 0)

    def merge(i, carry):
        for j in range(gs):
            r = pl.ds(pl.multiple_of((i * gs + j) * blk, blk), blk)
            ms = [m_ref[g, r, :] for g in range(n_win)]
            m_all = functools.reduce(jnp.maximum, ms)
            ws = [jnp.exp(m - m_all) for m in ms]
            num = functools.reduce(lambda a, b: a + b, [w * acc_ref[g, r, :] for g, w in enumerate(ws)])
            den = functools.reduce(lambda a, b: a + b, [w * l_ref[g, r, :] for g, w in enumerate(ws)])
            o_ref[r, :] = (num / den).astype(o_ref.dtype)
        return carry
    lax.fori_loop(0, nblk // gs, merge, 0)


def _attention(proj, slopes, cfg):
    seq, hd, nh = cfg.seq, cfg.head_dim, cfg.n_heads
    blks = {w // d for w, d in cfg.patterns}
    assert len(blks) == 1, "all windows must share one block length"
    blk = blks.pop()
    for w, d in cfg.patterns:
        nb = seq // (d * blk)
        assert d & (d - 1) == 0 and nb & (nb - 1) == 0 and nb * d * blk == seq
    grid_spec = pltpu.PrefetchScalarGridSpec(
        num_scalar_prefetch=1, grid=(cfg.batch, nh),
        in_specs=[pl.BlockSpec((seq, hd), lambda b, h, s: (b, h)),
                  pl.BlockSpec((seq, hd), lambda b, h, s: (b, nh + h)),
                  pl.BlockSpec((seq, hd), lambda b, h, s: (b, 2 * nh + h))],
        out_specs=pl.BlockSpec((seq, hd), lambda b, h, s: (b, h)),
        scratch_shapes=[pltpu.VMEM((len(cfg.patterns), seq, hd), F32)] * 3)
    return pl.pallas_call(
        functools.partial(_attn_kernel, cfg=cfg, blk=blk), grid_spec=grid_spec,
        out_shape=jax.ShapeDtypeStruct((cfg.tokens, cfg.attn_width), BF16),
        compiler_params=_params(("parallel", "parallel")), name="dilated_attention",
    )(slopes, proj, proj, proj)


def _conv_kernel(gb_ref, gc_ref, val_ref, w_ref, o_ref):
    u = gc_ref[...] * val_ref[...]
    row = lax.broadcasted_iota(jnp.int32, u.shape, 0)
    u1 = jnp.where(row >= 1, pltpu.roll(u, 1, 0), 0.0)
    u2 = jnp.where(row >= 2, pltpu.roll(u, 2, 0), 0.0)
    w = w_ref[...]
    z = w[0:1, :] * u2 + w[1:2, :] * u1 + w[2:3, :] * u
    o_ref[...] = (gb_ref[...] * z).astype(o_ref.dtype)


def _gated_conv(proj, conv_w, cfg):
    seq, cw, c = cfg.seq, cfg.conv_cw, cfg.conv_width
    base = 3 * cfg.attn_width // cw
    per = c // cw
    spec = lambda off: pl.BlockSpec((seq, cw), lambda b, j, off=off: (b, base + off * per + j))
    return pl.pallas_call(
        _conv_kernel, grid=(cfg.batch, per),
        in_specs=[spec(0), spec(1), spec(2), pl.BlockSpec((conv_w.shape[0], cw), lambda b, j: (0, j))],
        out_specs=pl.BlockSpec((seq, cw), lambda b, j: (b, j)),
        out_shape=jax.ShapeDtypeStruct((cfg.tokens, c), BF16),
        compiler_params=_params(("parallel", "parallel")), name="gated_conv",
    )(proj, proj, proj, conv_w)


def _ln1_router_kernel(x_ref, mix_ref, gi_ref, bi_ref, g1_ref, b1_ref, rw_ref, rb_ref,
                       h_ref, hp_ref, idx_ref, gate_ref, *, alpha, top_k):
    h0 = _layer_norm_rows(x_ref[...], gi_ref[...], bi_ref[...])
    h1 = _layer_norm_rows(alpha * h0 + mix_ref[...], g1_ref[...], b1_ref[...])
    h_ref[...] = h1
    hp_ref[...] = _pack_halves(h1)
    def split(v):
        hi = v.astype(BF16)
        return hi, (v - hi.astype(F32)).astype(BF16)

    def dot_t(a, b):
        return lax.dot_general(a, b, (((1,), (1,)), ((), ())), preferred_element_type=F32)
    w_hi, w_lo = split(rw_ref[...])
    h_hi, h_lo = split(h1)
    logits = dot_t(w_hi, h_hi) + (dot_t(w_hi, h_lo) + dot_t(w_lo, h_hi)) + rb_ref[...]
    n_e = logits.shape[0]
    eidx = lax.broadcasted_iota(jnp.int32, logits.shape, 0)
    vals, idxs = [], []
    for _ in range(top_k):
        m = jnp.max(logits, axis=0, keepdims=True)
        i = jnp.min(jnp.where(logits == m, eidx, n_e), axis=0, keepdims=True)
        vals.append(m)
        idxs.append(i)
        logits = jnp.where(eidx == i, -jnp.inf, logits)
    ex = [jnp.exp(v - vals[0]) for v in vals]
    den = functools.reduce(lambda a, b: a + b, ex)
    idx_ref[...] = jnp.concatenate(idxs, axis=0)
    gate_ref[...] = jnp.concatenate([e / den for e in ex], axis=0)


def _ln1_router(x2d, mix, gi, bi, g1, b1, router_w, router_b, cfg):
    t, d = x2d.shape
    e, k = cfg.n_experts, cfg.top_k
    tm = cfg.ln_rows
    alpha = (2 * cfg.depth) ** 0.25
    row = pl.BlockSpec((tm, d), lambda i: (i, 0))
    vec = pl.BlockSpec((1, d), lambda i: (0, 0))
    sel = pl.BlockSpec((k, tm), lambda i: (0, i))
    return pl.pallas_call(
        functools.partial(_ln1_router_kernel, alpha=alpha, top_k=k), grid=(t // tm,),
        in_specs=[row, row, vec, vec, vec, vec,
                  pl.BlockSpec((e, d), lambda i: (0, 0)), pl.BlockSpec((e, 1), lambda i: (0, 0))],
        out_specs=[row, pl.BlockSpec((tm, d // 2), lambda i: (i, 0)), sel, sel],
        out_shape=[jax.ShapeDtypeStruct((t, d), F32), jax.ShapeDtypeStruct((t, d // 2), jnp.int32),
                   jax.ShapeDtypeStruct((k, t), jnp.int32),
                   jax.ShapeDtypeStruct((k, t), F32)],
        compiler_params=_params(("parallel",)), name="ln1_router",
    )(x2d, mix, gi.reshape(1, d), bi.reshape(1, d), g1.reshape(1, d), b1.reshape(1, d),
      router_w.T, router_b.reshape(e, 1))


def _routing_plan(top_idx, cfg):
    k, t = top_idx.shape
    e, rb = cfg.n_experts, cfg.moe_rows
    a = k * t
    nb = a // rb + e
    p = nb * rb
    i32 = jnp.int32
    flat_e = top_idx.reshape(a)
    slot = jnp.arange(a, dtype=i32)
    experts = jnp.arange(e, dtype=i32)
    _, order = lax.sort((flat_e, slot), num_keys=1)
    counts = jnp.sum((flat_e[:, None] == experts[None, :]).astype(i32), axis=0)
    ends = jnp.cumsum(counts)
    padded = (counts + rb - 1) // rb * rb
    pends = jnp.cumsum(padded)
    pstarts = pends - padded
    gap = padded - counts

    def gaps_before(x, bounds):
        return jnp.sum(jnp.where(x[:, None] >= bounds[None, :], gap[None, :], 0), axis=1)
    pos_sorted = slot + gaps_before(slot, ends)
    _, pos = lax.sort((order, pos_sorted), num_keys=1)

    rows = jnp.arange(p, dtype=i32)
    src = rows - gaps_before(rows, pends)
    seg_end = jnp.sum(jnp.where(rows[:, None] >= pstarts[None, :], counts[None, :], 0), axis=1)
    row_tok = jnp.where(src < seg_end, order[jnp.clip(src, 0, a - 1)] % t, 0).astype(i32)

    n_blocks = pends[-1] // rb
    bi = jnp.arange(nb, dtype=i32)
    raw_e = jnp.minimum(jnp.sum((bi[:, None] * rb >= pends[None, :]).astype(i32), axis=1), e - 1)
    block_e = jnp.where(bi < n_blocks, raw_e, raw_e[n_blocks - 1])
    first = jnp.concatenate([jnp.ones((1,), i32), (block_e[1:] != block_e[:-1]).astype(i32)])
    tile_ord = jnp.cumsum(first) - 1
    tile_e = jnp.max(jnp.where(tile_ord[None, :] == bi[:, None], block_e[None, :], -1), axis=1)
    next_e = tile_e[jnp.minimum(tile_ord + 1, nb - 1)]
    meta = jnp.stack([n_blocks, tile_ord[-1] + 1]).astype(i32)
    return (row_tok, pos.reshape(k, t).astype(i32), block_e.astype(i32), first, tile_ord.astype(i32),
            next_e.astype(i32), meta)


ROW_DMA_UNROLL = 8


def _row_copy(src_hbm, dst_vmem, sem, src_row, dst_row):
    return pltpu.make_async_copy(src_hbm.at[pl.ds(src_row, 1), :], dst_vmem.at[pl.ds(dst_row, 1), :], sem)


def _start_rows(src_hbm, dst_vmem, sem, n, src_row_of, inline=False, queues=(0, 1)):
    if inline:
        for r in range(n):
            _row_copy(src_hbm, dst_vmem, sem, src_row_of(r), r).start(priority=queues[r % len(queues)])
        return

    def body(c, carry):
        for u in range(ROW_DMA_UNROLL):
            r = c * ROW_DMA_UNROLL + u
            _row_copy(src_hbm, dst_vmem, sem, src_row_of(r), r).start(priority=queues[u % len(queues)])
        return carry
    lax.fori_loop(0, n // ROW_DMA_UNROLL, body, 0)


def _wait_rows(src_hbm, dst_vmem, sem, n):
    def body(c, carry):
        for u in range(ROW_DMA_UNROLL):
            _row_copy(src_hbm, dst_vmem, sem, 0, c * ROW_DMA_UNROLL + u).wait()
        return carry
    lax.fori_loop(0, n // ROW_DMA_UNROLL, body, 0)


WEIGHT_CAST_CHUNKS = 4
WEIGHT_DMA_PRIORITY = 1
ROW_SLOTS = 3

class _TilePlan:
    def __init__(self, be_ref, first_ref, ord_ref, next_ref, meta_ref):
        j, i = pl.program_id(0), pl.program_id(1)
        self.j, self.i = j, i
        self.expert = be_ref[i]
        self.first = first_ref[i] == 1
        self.valid = i < meta_ref[0]
        tile = j * meta_ref[1] + ord_ref[i]
        self.slot = tile & 1
        self.is_tile0 = tile == 0
        nxt = next_ref[i]
        more_j = j + 1 < pl.num_programs(0)
        self.has_next = jnp.logical_or(nxt >= 0, more_j)
        self.next_e = jnp.where(nxt >= 0, nxt, be_ref[0])
        self.next_j = jnp.where(nxt >= 0, j, j + 1)


def _first_step_dot(x, stage, w_s):
    k = x.shape[1]
    ck = k // WEIGHT_CAST_CHUNKS
    acc = None
    for c in range(WEIGHT_CAST_CHUNKS):
        wb = stage[pl.ds(c * ck, ck), :].astype(BF16)
        w_s[pl.ds(c * ck, ck), :] = wb
        part = jnp.dot(x[:, c * ck:(c + 1) * ck], wb, preferred_element_type=F32)
        acc = part if acc is None else acc + part
    return acc


def _swiglu(g, u):
    g = jnp.minimum(g, SWIGLU_LIMIT)
    u = jnp.clip(u, -SWIGLU_LIMIT, SWIGLU_LIMIT)
    return (u + 1.0) * (g * jax.nn.sigmoid(SWIGLU_ALPHA * g))


def _gate_up_kernel(be_ref, first_ref, ord_ref, next_ref, meta_ref, *refs, tf, f, j0, gather):
    if gather:
        (tok_ref, tok1_ref, tok2_ref, h_hbm, w_hbm, bg_ref, bu_ref, x_ref, o_ref,
         rows, stage_g, stage_u, wg_s, wu_s, row_sem, sem) = refs
    else:
        x_ref, w_hbm, bg_ref, bu_ref, o_ref, stage_g, stage_u, wg_s, wu_s, sem = refs
    plan = _TilePlan(be_ref, first_ref, ord_ref, next_ref, meta_ref)

    if gather:
        n, half = rows.shape[1], rows.shape[2]
        live = meta_ref[0]
        slot = lax.rem(plan.i, ROW_SLOTS)
        ahead = lax.rem(plan.i + 2, ROW_SLOTS)

        @pl.when(plan.i == 0)
        def _():
            _start_rows(h_hbm, rows.at[0], row_sem.at[0], n, lambda r: tok_ref[0, r])

        @pl.when(jnp.logical_and(plan.i == 0, live > 1))
        def _():
            _start_rows(h_hbm, rows.at[1], row_sem.at[1], n, lambda r: tok1_ref[0, r])

        @pl.when(plan.i + 2 < live)
        def _():
            _start_rows(h_hbm, rows.at[ahead], row_sem.at[ahead], n, lambda r: tok2_ref[0, r])

        @pl.when(plan.valid)
        def _():
            _wait_rows(h_hbm, rows.at[slot], row_sem.at[slot], n)
            words = rows[slot]
            x_ref[:, :half] = _unpack_half(words, 0).astype(x_ref.dtype)
            x_ref[:, half:] = _unpack_half(words, 1).astype(x_ref.dtype)

        @pl.when(jnp.logical_not(plan.valid))
        def _():
            x_ref[...] = jnp.zeros(x_ref.shape, x_ref.dtype)

    def tile_copies(e, j, slot):
        col = pl.multiple_of((j + j0) * tf, tf)
        return (pltpu.make_async_copy(w_hbm.at[e, :, pl.ds(col, tf)], stage_g.at[slot], sem.at[slot]),
                pltpu.make_async_copy(w_hbm.at[e, :, pl.ds(f + col, tf)], stage_u.at[slot], sem.at[slot]))

    @pl.when(plan.first)
    def _():
        @pl.when(plan.is_tile0)
        def _():
            for c in tile_copies(plan.expert, plan.j, plan.slot):
                c.start(priority=WEIGHT_DMA_PRIORITY)
        for c in tile_copies(plan.expert, plan.j, plan.slot):
            c.wait()

        @pl.when(plan.has_next)
        def _():
            for c in tile_copies(plan.next_e, plan.next_j, 1 - plan.slot):
                c.start(priority=WEIGHT_DMA_PRIORITY)
        x = x_ref[...]
        g = _first_step_dot(x, stage_g.at[plan.slot], wg_s) + bg_ref[...]
        u = _first_step_dot(x, stage_u.at[plan.slot], wu_s) + bu_ref[...]
        o_ref[...] = _swiglu(g, u).astype(o_ref.dtype)

    @pl.when(jnp.logical_and(jnp.logical_not(plan.first), plan.valid))
    def _():
        x = x_ref[...]
        g = jnp.dot(x, wg_s[...], preferred_element_type=F32) + bg_ref[...]
        u = jnp.dot(x, wu_s[...], preferred_element_type=F32) + bu_ref[...]
        o_ref[...] = _swiglu(g, u).astype(o_ref.dtype)

    @pl.when(jnp.logical_not(plan.valid))
    def _():
        o_ref[...] = jnp.zeros(o_ref.shape, o_ref.dtype)


def _gate_up(x_src, w_gu, b_gu, plan_arrays, cfg, j0, nj, row_tok=None):
    gather = row_tok is not None
    e, f, rb, tf = cfg.n_experts, cfg.d_expert, cfg.moe_rows, cfg.gu_tf
    d = cfg.d_model
    nf = f // tf
    p = row_tok.shape[0] if gather else x_src.shape[0]
    nb = p // rb
    assert d % (WEIGHT_CAST_CHUNKS * 128) == 0 and rb % ROW_DMA_UNROLL == 0
    bias = lambda off: pl.BlockSpec((None, 1, tf), lambda j, i, be, *_: (be[i], 0, off + j0 + j))
    act_spec = pl.BlockSpec((rb, tf), lambda j, i, be, *_: (i, j))
    row_spec = pl.BlockSpec((rb, d), lambda j, i, be, *_: (i, 0))
    weights = [pltpu.VMEM((2, d, tf), F32), pltpu.VMEM((2, d, tf), F32),
               pltpu.VMEM((d, tf), BF16), pltpu.VMEM((d, tf), BF16)]
    act_shape = jax.ShapeDtypeStruct((p, nj * tf), BF16)
    b3 = b_gu.reshape(e, 1, 2 * f)
    if gather:
        assert nj == 1
        tok = row_tok.reshape(nb, 1, rb)
        idx = lambda g: pl.BlockSpec((None, 1, rb), lambda j, i, *_: (g(i), 0, 0), memory_space=pltpu.SMEM)
        in_specs = [idx(lambda i: i), idx(lambda i: jnp.minimum(i + 1, nb - 1)),
                    idx(lambda i: jnp.minimum(i + 2, nb - 1)),
                    pl.BlockSpec(memory_space=pl.ANY), pl.BlockSpec(memory_space=pl.ANY), bias(0), bias(nf)]
        out_specs = [row_spec, act_spec]
        out_shape = [jax.ShapeDtypeStruct((p, d), BF16), act_shape]
        scratch = ([pltpu.VMEM((ROW_SLOTS, rb, d // 2), jnp.int32)] + weights
                   + [pltpu.SemaphoreType.DMA((ROW_SLOTS,)), pltpu.SemaphoreType.DMA((2,))])
        operands = (tok, tok, tok, x_src, w_gu, b3, b3)
    else:
        in_specs = [row_spec, pl.BlockSpec(memory_space=pl.ANY), bias(0), bias(nf)]
        out_specs, out_shape = act_spec, act_shape
        scratch = weights + [pltpu.SemaphoreType.DMA((2,))]
        operands = (x_src, w_gu, b3, b3)
    grid_spec = pltpu.PrefetchScalarGridSpec(
        num_scalar_prefetch=5, grid=(nj, nb), in_specs=in_specs, out_specs=out_specs, scratch_shapes=scratch)
    return pl.pallas_call(
        functools.partial(_gate_up_kernel, tf=tf, f=f, j0=j0, gather=gather), grid_spec=grid_spec,
        out_shape=out_shape, compiler_params=_params(("arbitrary", "arbitrary")),
        name="moe_gather_gate_up" if gather else "moe_gate_up",
    )(*plan_arrays, *operands)


def _down_kernel(be_ref, first_ref, ord_ref, next_ref, meta_ref, *refs, tn):
    *a_refs, w_hbm, b_ref, o_ref, stage, w_s, sem = refs
    plan = _TilePlan(be_ref, first_ref, ord_ref, next_ref, meta_ref)

    def act():
        return jnp.concatenate([r[...] for r in a_refs], axis=1) if len(a_refs) > 1 else a_refs[0][...]

    def tile_copy(e, j, slot):
        col = pl.multiple_of(j * tn, tn)
        return pltpu.make_async_copy(w_hbm.at[e, :, pl.ds(col, tn)], stage.at[slot], sem.at[slot])

    @pl.when(plan.first)
    def _():
        @pl.when(plan.is_tile0)
        def _():
            tile_copy(plan.expert, plan.j, plan.slot).start(priority=WEIGHT_DMA_PRIORITY)
        tile_copy(plan.expert, plan.j, plan.slot).wait()

        @pl.when(plan.has_next)
        def _():
            tile_copy(plan.next_e, plan.next_j, 1 - plan.slot).start(priority=WEIGHT_DMA_PRIORITY)
        o_ref[...] = _pack_halves(_first_step_dot(act(), stage.at[plan.slot], w_s) + b_ref[...])

    @pl.when(jnp.logical_and(jnp.logical_not(plan.first), plan.valid))
    def _():
        o_ref[...] = _pack_halves(jnp.dot(act(), w_s[...], preferred_element_type=F32) + b_ref[...])

    @pl.when(jnp.logical_not(plan.valid))
    def _():
        o_ref[...] = jnp.zeros(o_ref.shape, o_ref.dtype)


def _down(acts, w_dn, b_dn, plan_arrays, cfg):
    p = acts[0].shape[0]
    f = sum(a.shape[1] for a in acts)
    e, d, rb, tn = cfg.n_experts, cfg.d_model, cfg.moe_rows, cfg.dn_tn
    assert f == cfg.d_expert and f % (WEIGHT_CAST_CHUNKS * 128) == 0
    grid_spec = pltpu.PrefetchScalarGridSpec(
        num_scalar_prefetch=5, grid=(d // tn, p // rb),
        in_specs=[pl.BlockSpec((rb, a.shape[1]), lambda j, i, be, *_: (i, 0)) for a in acts] + [
                  pl.BlockSpec(memory_space=pl.ANY),
                  pl.BlockSpec((None, 1, tn), lambda j, i, be, *_: (be[i], 0, j))],
        out_specs=pl.BlockSpec((rb, tn // 2), lambda j, i, be, *_: (i, j)),
        scratch_shapes=[pltpu.VMEM((2, f, tn), F32), pltpu.VMEM((f, tn), BF16),
                        pltpu.SemaphoreType.DMA((2,))])
    return pl.pallas_call(
        functools.partial(_down_kernel, tn=tn), grid_spec=grid_spec,
        out_shape=jax.ShapeDtypeStruct((p, d // 2), jnp.int32),
        compiler_params=_params(("arbitrary", "arbitrary")), name="moe_down",
    )(*plan_arrays, *acts, w_dn, b_dn.reshape(e, 1, d))


def _combine_kernel(pos_ref, pos_next_ref, ys_hbm, h_ref, gate_ref, g2_ref, b2_ref, o_ref, buf, sem, *,
                    alpha, tile):
    i = pl.program_id(0)
    top_k, n = buf.shape[1], buf.shape[2]
    slot = i % 2
    hw = tile // 2

    def start(idx_ref, s, inline):
        for k in range(top_k):
            _start_rows(ys_hbm, buf.at[s, k], sem.at[s], n, lambda t, k=k: idx_ref[k, t], inline=inline)

    def wait(s):
        for k in range(top_k):
            _wait_rows(ys_hbm, buf.at[s, k], sem.at[s], n)

    @pl.when(i == 0)
    def _():
        start(pos_ref, 0, inline=False)

    wait(slot)
    start(pos_next_ref, 1 - slot, inline=True)
    gates = gate_ref[...]
    slabs = []
    for j in range(buf.shape[3] // hw):
        words = [buf[slot, k, :, j * hw:(j + 1) * hw] for k in range(top_k)]
        for half in (0, 1):
            terms = [gates[:, k:k + 1] * _unpack_half(words[k], half) for k in range(top_k)]
            slabs.append(functools.reduce(lambda a, b: a + b, terms))
    y = jnp.concatenate(slabs, axis=1)
    o_ref[...] = _layer_norm_rows(alpha * h_ref[...] + y, g2_ref[...], b2_ref[...])

    @pl.when(i == pl.num_programs(0) - 1)
    def _():
        wait(1 - slot)


def _combine(ys, pos, gates_tk, h1, g2, b2, cfg):
    t, d = h1.shape
    k, n = cfg.top_k, cfg.combine_rows
    steps = t // n
    assert n % ROW_DMA_UNROLL == 0
    alpha = (2 * cfg.depth) ** 0.25
    pos_blocks = pos.reshape(k, steps, n).transpose(1, 0, 2)
    idx = lambda f: pl.BlockSpec((None, k, n), lambda i: (f(i), 0, 0), memory_space=pltpu.SMEM)
    row = pl.BlockSpec((n, d), lambda i: (i, 0))
    vec = pl.BlockSpec((1, d), lambda i: (0, 0))
    return pl.pallas_call(
        functools.partial(_combine_kernel, alpha=alpha, tile=cfg.dn_tn), grid=(steps,),
        in_specs=[idx(lambda i: i), idx(lambda i: jnp.minimum(i + 1, steps - 1)),
                  pl.BlockSpec(memory_space=pl.ANY), row,
                  pl.BlockSpec((n, k), lambda i: (i, 0)), vec, vec],
        out_specs=row,
        out_shape=jax.ShapeDtypeStruct((t, d), F32),
        scratch_shapes=[pltpu.VMEM((2, k, n, d // 2), jnp.int32), pltpu.SemaphoreType.DMA((2,))],
        compiler_params=_params(("arbitrary",)), name="moe_combine_ln2",
    )(pos_blocks, pos_blocks, ys, h1, gates_tk, g2.reshape(1, d), b2.reshape(1, d))


def _forward(cfg, x, ln_in_g, ln_in_b, w_in, conv_w, w_out, ln1_g, ln1_b, router_w, router_b,
             w_gate_up, b_gate_up, w_down, b_down, ln2_g, ln2_b):
    assert cfg.depth == 1
    b, s, d = x.shape
    x2d = x.reshape(b * s, d)
    slopes = jnp.asarray(_alibi_slopes(cfg.n_heads))
    h0 = _ln_in(x2d, ln_in_g, ln_in_b, cfg)
    proj = _matmul([h0], w_in[0], cfg, F32, "in_proj")
    attn = _attention(proj, slopes, cfg)
    y_conv = _gated_conv(proj, conv_w[0], cfg)
    mix = _matmul([attn, y_conv], w_out[0], cfg, F32, "out_proj")
    h1, h1_packed, top_idx, gates = _ln1_router(x2d, mix, ln_in_g, ln_in_b, ln1_g[0], ln1_b[0],
                                                router_w[0], router_b[0], cfg)
    row_tok, pos, *plan_arrays = _routing_plan(top_idx, cfg)
    n_tiles = cfg.d_expert // cfg.gu_tf
    xs, act0 = _gate_up(h1_packed, w_gate_up[0], b_gate_up[0], plan_arrays, cfg, 0, 1, row_tok=row_tok)
    acts = [act0]
    if n_tiles > 1:
        acts.append(_gate_up(xs, w_gate_up[0], b_gate_up[0], plan_arrays, cfg, 1, n_tiles - 1))
    ys = _down(acts, w_down[0], b_down[0], plan_arrays, cfg)
    out = _combine(ys, pos, gates.T, h1, ln2_g[0], ln2_b[0], cfg)
    return out.reshape(b, s, d)


_CONFIG = Config(
    batch=4, seq=2048, d_model=4096, head_dim=128, n_heads=24, conv_width=1024,
    patterns=((128, 1), (512, 4), (2048, 16)), n_experts=32, top_k=4, d_expert=2048, depth=1,
    ln_rows=256, mm_tm=1024, mm_tn=512, conv_cw=256, moe_rows=256, gu_tf=512, dn_tn=2048,
    gather_rows=256, combine_rows=128)


def kernel(x, ln_in_g, ln_in_b, w_in, conv_w, w_out, ln1_g, ln1_b, router_w, router_b,
           w_gate_up, b_gate_up, w_down, b_down, ln2_g, ln2_b):
    return _forward(_CONFIG, x, ln_in_g, ln_in_b, w_in, conv_w, w_out, ln1_g, ln1_b, router_w, router_b,
                    w_gate_up, b_gate_up, w_down, b_down, ln2_g, ln2_b)
```
